```python
import math
import jax, jax.numpy as jnp
from jax import lax
import numpy as np

D_MODEL = 2048
BATCH = 4
SEQ = 2048
DEPTH = 4
DEC_BATCH = 128
DEC_SEQ = 1
PAST_LEN = 16384
PAGE_SIZE = 128

N_MIXERS = 2
N_POOL_LAYERS = (DEPTH + N_MIXERS - 1) // N_MIXERS
N_RET_LAYERS = DEPTH // N_MIXERS
POOL_WINDOWS = (2, 4, 8, 16)
POOL_GROUPS = len(POOL_WINDOWS)
POOL_GROUP_DIM = D_MODEL // POOL_GROUPS
POOL_BUF = max(POOL_WINDOWS) - 1
RET_HEADS = 8
RET_DK = D_MODEL // RET_HEADS
RET_DV = 2 * D_MODEL // RET_HEADS
RET_CHUNK = 128
RET_HK = RET_HEADS * RET_DK
RET_HV = RET_HEADS * RET_DV
RET_IN_DIM = 2 * RET_HK + 2 * RET_HV
ROPE_BASE = 10000.0
N_EXPERTS = 32
TOP_K = 4
D_FF = D_MODEL
SWIGLU_LIMIT = 7.0
SWIGLU_ALPHA = 1.702
MOE_BLOCK = 64
NORM_EPS = 1e-5

kernel_name = 'hybrid_pool_retention_moe_step'


def rmsnorm(x, g):
    xf = x.astype(jnp.float32)
    y = xf * lax.rsqrt(jnp.mean(xf * xf, axis=-1, keepdims=True) + NORM_EPS)
    return (y * g.astype(jnp.float32)).astype(x.dtype)


def adaln(c, w, b):
    m = (jax.nn.silu(c) @ w + b)[:, None, :]
    return jnp.split(m, 6, axis=-1)


def modulate(h, shift, scale):
    return h * (1.0 + scale) + shift


def pool_mixer(h_ext, n_prev, w_grp, scale):
    B, L_ext, D = h_ext.shape
    L = L_ext - n_prev
    hf = h_ext.astype(jnp.float32)
    cs = jnp.concatenate([jnp.zeros((B, 1, D), jnp.float32), jnp.cumsum(hf, axis=1)], axis=1)
    hi = np.arange(n_prev, L_ext) + 1
    means = []
    for gi, w in enumerate(POOL_WINDOWS):
        lo = np.maximum(hi - w, 0)
        cnt = jnp.asarray((hi - lo).astype(np.float32))[None, :, None]
        sl = slice(gi * POOL_GROUP_DIM, (gi + 1) * POOL_GROUP_DIM)
        means.append((cs[:, hi, sl] - cs[:, lo, sl]) / cnt)
    diff = (jnp.concatenate(means, axis=-1) - hf[:, n_prev:]).astype(h_ext.dtype)
    y = jnp.einsum('blgc,gcd->blgd', diff.reshape(B, L, POOL_GROUPS, POOL_GROUP_DIM), w_grp)
    return y.reshape(B, L, D) * scale


def rope(x, pos):
    inv = ROPE_BASE ** (-jnp.linspace(0.0, 1.0, RET_DK // 2, dtype=jnp.float32))
    ang = pos.astype(jnp.float32)[:, None] * inv[None, :]
    cos = jnp.cos(ang)[None, :, None, :]
    sin = jnp.sin(ang)[None, :, None, :]
    x1, x2 = jnp.split(x, 2, axis=-1)
    return jnp.concatenate([x1 * cos - x2 * sin, x1 * sin + x2 * cos], axis=-1)


def retention(h, w_in, w_out, state0, pos0):
    B, L, _ = h.shape
    proj = h @ w_in
    q, k, v, g = jnp.split(proj, [RET_HK, 2 * RET_HK, 2 * RET_HK + RET_HV], axis=-1)
    pos = pos0 + jnp.arange(L)
    q = rope(q.reshape(B, L, RET_HEADS, RET_DK).astype(jnp.float32), pos)
    k = rope(k.reshape(B, L, RET_HEADS, RET_DK).astype(jnp.float32), pos) * (RET_DK ** -0.5)
    v = v.reshape(B, L, RET_HEADS, RET_DV).astype(jnp.float32)
    C = L if L <= RET_CHUNK else math.gcd(L, RET_CHUNK)
    n = L // C

    def to_chunks(t):
        return t.reshape(B, n, C, RET_HEADS, t.shape[-1]).transpose(1, 0, 3, 2, 4)

    log_g = jnp.log(1.0 - 2.0 ** (-5.0 - jnp.arange(RET_HEADS, dtype=jnp.float32)))
    i = jnp.arange(C, dtype=jnp.float32)
    rel = i[:, None] - i[None, :]
    intra = jnp.where(rel >= 0, jnp.exp(log_g[:, None, None] * jnp.maximum(rel, 0.0)), 0.0)
    q_dec = jnp.exp(log_g[:, None] * (i + 1.0))[None, :, :, None]
    k_dec = jnp.exp(log_g[:, None] * (C - 1.0 - i))[None, :, :, None]
    c_dec = jnp.exp(log_g * C)[None, :, None, None]

    def step(S, inp):
        qc, kc, vc = inp
        s = jnp.einsum('bhid,bhjd->bhij', qc, kc) * intra
        o = jnp.einsum('bhij,bhjv->bhiv', s, vc) + jnp.einsum('bhid,bhdv->bhiv', qc * q_dec, S)
        S = c_dec * S + jnp.einsum('bhjd,bhjv->bhdv', kc * k_dec, vc)
        return S, o

    S_fin, o = lax.scan(step, state0.astype(jnp.float32), (to_chunks(q), to_chunks(k), to_chunks(v)))
    o = o.transpose(1, 0, 3, 2, 4).reshape(B, L, RET_HEADS, RET_DV)
    o = o * lax.rsqrt(jnp.mean(o * o, axis=-1, keepdims=True) + NORM_EPS)
    y = (jax.nn.silu(g) * o.reshape(B, L, RET_HV).astype(h.dtype)) @ w_out
    return y, S_fin


def moe_ffn(x, w_router, b_router, w_gate_up, b_gate_up, w_down, b_down):
    T, D = x.shape
    logits = x.astype(jnp.float32) @ w_router.astype(jnp.float32) + b_router.astype(jnp.float32)
    top_val, top_idx = lax.top_k(logits, TOP_K)
    gates = jax.nn.softmax(top_val, axis=-1)
    A = T * TOP_K
    flat_e = top_idx.reshape(A)
    flat_tok = jnp.repeat(jnp.arange(T, dtype=jnp.int32), TOP_K)
    flat_g = gates.reshape(A)
    order = jnp.argsort(flat_e, stable=True)
    se, stok, sg = flat_e[order], flat_tok[order], flat_g[order]
    counts = jnp.bincount(flat_e, length=N_EXPERTS)
    starts = jnp.cumsum(counts) - counts
    padded = (counts + MOE_BLOCK - 1) // MOE_BLOCK * MOE_BLOCK
    pad_ends = jnp.cumsum(padded)
    pad_starts = pad_ends - padded
    dest = pad_starts[se] + jnp.arange(A) - starts[se]
    n_blocks = -(-(A + N_EXPERTS * (MOE_BLOCK - 1)) // MOE_BLOCK)
    R = n_blocks * MOE_BLOCK
    row_tok = jnp.zeros((R,), jnp.int32).at[dest].set(stok)
    row_gate = jnp.zeros((R,), jnp.float32).at[dest].set(sg)
    blk_expert = jnp.minimum(jnp.searchsorted(pad_ends, jnp.arange(n_blocks) * MOE_BLOCK, side='right'), N_EXPERTS - 1)
    xb = x[row_tok].reshape(n_blocks, MOE_BLOCK, D)

    def expert_block(args):
        xe, e = args
        gu = xe @ w_gate_up[e] + b_gate_up[e]
        gt, up = jnp.split(gu, 2, axis=-1)
        gt = jnp.minimum(gt, SWIGLU_LIMIT)
        up = jnp.clip(up, -SWIGLU_LIMIT, SWIGLU_LIMIT)
        hdn = (up + 1.0) * gt * jax.nn.sigmoid(SWIGLU_ALPHA * gt)
        return hdn @ w_down[e] + b_down[e]

    yb = lax.map(expert_block, (xb, blk_expert)).reshape(R, D)
    return jnp.zeros((T, D), x.dtype).at[row_tok].add(yb * row_gate[:, None].astype(yb.dtype))


def setup_inputs(seed: int = 0) -> dict:
    key = jax.random.key(seed)
    ks = jax.random.split(key, 21)

    def nrm(k, shape, scale=1.0):
        return jax.random.normal(k, shape, jnp.float32) * scale

    D = D_MODEL
    return {
        'x_prompt': nrm(ks[0], (BATCH, SEQ, D)),
        'x_sample': nrm(ks[1], (DEC_BATCH, DEC_SEQ, D)),
        'state_pool': nrm(ks[2], (N_POOL_LAYERS, DEC_BATCH, POOL_BUF, D)),
        'state_ret': nrm(ks[3], (N_RET_LAYERS, DEC_BATCH, RET_HEADS, RET_DK, RET_DV), 0.05),
        'c_prompt': nrm(ks[4], (BATCH, D)),
        'c_sample': nrm(ks[5], (DEC_BATCH, D)),
        'w_mod': nrm(ks[6], (DEPTH, D, 6 * D), 0.5 * D ** -0.5),
        'b_mod': nrm(ks[7], (DEPTH, 6 * D), 0.02),
        'g_norm_mix': 1.0 + nrm(ks[8], (DEPTH, D), 0.02),
        'g_norm_ffn': 1.0 + nrm(ks[9], (DEPTH, D), 0.02),
        'pool_w': nrm(ks[10], (N_POOL_LAYERS, POOL_GROUPS, POOL_GROUP_DIM, POOL_GROUP_DIM), POOL_GROUP_DIM ** -0.5),
        'pool_scale': 1.0 + nrm(ks[11], (N_POOL_LAYERS, D), 0.02),
        'ret_w_in': nrm(ks[12], (N_RET_LAYERS, D, RET_IN_DIM), D ** -0.5),
        'ret_w_out': nrm(ks[13], (N_RET_LAYERS, RET_HV, D), RET_HV ** -0.5),
        'moe_w_router': nrm(ks[14], (DEPTH, D, N_EXPERTS), D ** -0.5),
        'moe_b_router': nrm(ks[15], (DEPTH, N_EXPERTS), 0.01),
        'moe_w_gate_up': nrm(ks[16], (DEPTH, N_EXPERTS, D, 2 * D_FF), D ** -0.5),
        'moe_b_gate_up': nrm(ks[17], (DEPTH, N_EXPERTS, 2 * D_FF), 0.02),
        'moe_w_down': nrm(ks[18], (DEPTH, N_EXPERTS, D_FF, D), D_FF ** -0.5),
        'moe_b_down': nrm(ks[19], (DEPTH, N_EXPERTS, D), 0.02),
        'g_final': 1.0 + nrm(ks[20], (D,), 0.02),
    }


def reference(x_prompt, x_sample, state_pool, state_ret, c_prompt, c_sample, w_mod, b_mod, g_norm_mix, g_norm_ffn, pool_w, pool_scale, ret_w_in, ret_w_out, moe_w_router, moe_b_router, moe_w_gate_up, moe_b_gate_up, moe_w_down, moe_b_down, g_final):
    xp, xs = x_prompt, x_sample
    new_pool_p, new_pool_s, new_ret_p, new_ret_s = [], [], [], []
    for layer in range(DEPTH):
        j = layer // N_MIXERS
        sp = adaln(c_prompt, w_mod[layer], b_mod[layer])
        ss = adaln(c_sample, w_mod[layer], b_mod[layer])
        hp = modulate(rmsnorm(xp, g_norm_mix[layer]), sp[0], sp[1])
        hs = modulate(rmsnorm(xs, g_norm_mix[layer]), ss[0], ss[1])
        if layer % N_MIXERS == 0:
            yp = pool_mixer(hp, 0, pool_w[j], pool_scale[j])
            ext = jnp.concatenate([state_pool[j].astype(hs.dtype), hs], axis=1)
            ys = pool_mixer(ext, POOL_BUF, pool_w[j], pool_scale[j])
            new_pool_p.append(hp[:, -POOL_BUF:])
            new_pool_s.append(ext[:, -POOL_BUF:])
        else:
            s0 = jnp.zeros((xp.shape[0], RET_HEADS, RET_DK, RET_DV), jnp.float32)
            yp, Sp = retention(hp, ret_w_in[j], ret_w_out[j], s0, 0)
            ys, Ss = retention(hs, ret_w_in[j], ret_w_out[j], state_ret[j], PAST_LEN)
            new_ret_p.append(Sp)
            new_ret_s.append(Ss)
        xp = xp + sp[2] * yp
        xs = xs + ss[2] * ys
        hp = modulate(rmsnorm(xp, g_norm_ffn[layer]), sp[3], sp[4])
        hs = modulate(rmsnorm(xs, g_norm_ffn[layer]), ss[3], ss[4])
        fp = moe_ffn(hp.reshape(-1, D_MODEL), moe_w_router[layer], moe_b_router[layer], moe_w_gate_up[layer], moe_b_gate_up[layer], moe_w_down[layer], moe_b_down[layer])
        fs = moe_ffn(hs.reshape(-1, D_MODEL), moe_w_router[layer], moe_b_router[layer], moe_w_gate_up[layer], moe_b_gate_up[layer], moe_w_down[layer], moe_b_down[layer])
        xp = xp + sp[5] * fp.reshape(xp.shape)
        xs = xs + ss[5] * fs.reshape(xs.shape)
    y_prompt = rmsnorm(xp, g_final)
    y_sample = rmsnorm(xs, g_final)
    return (y_prompt, y_sample, jnp.stack(new_pool_p), jnp.stack(new_pool_s), jnp.stack(new_ret_p), jnp.stack(new_ret_s))
```

```python
import functools

import jax
import jax.numpy as jnp
import numpy as np
from jax import lax
from jax.experimental import pallas as pl
from jax.experimental.pallas import tpu as pltpu

F32 = jnp.float32
BF16 = jnp.bfloat16
I32 = jnp.int32
U32 = jnp.uint32

D_MODEL = 2048
BATCH = 4
SEQ = 2048
DEPTH = 4
DEC_BATCH = 128
PAST_LEN = 16384
N_MIXERS = 2
POOL_WINDOWS = (2, 4, 8, 16)
POOL_GROUP_DIM = D_MODEL // len(POOL_WINDOWS)
POOL_BUF = max(POOL_WINDOWS) - 1
POOL_HALO = 16
RET_HEADS = 8
RET_DK = D_MODEL // RET_HEADS
RET_DV = 2 * D_MODEL // RET_HEADS
RET_CHUNK = 128
RET_HK = RET_HEADS * RET_DK
RET_HV = RET_HEADS * RET_DV
RET_IN_DIM = 2 * RET_HK + 2 * RET_HV
ROPE_BASE = 10000.0
N_EXPERTS = 32
TOP_K = 4
D_FF = D_MODEL
SWIGLU_LIMIT = 7.0
SWIGLU_ALPHA = 1.702
NORM_EPS = 1e-5

TP = BATCH * SEQ
T_ALL = TP + DEC_BATCH
TOK_TILE = 128
N_TOK_TILES = T_ALL // TOK_TILE
N_PROMPT_TILES = TP // TOK_TILE
TILES_PER_SEQ = SEQ // TOK_TILE
LANES = 128

MOE_SUB = 256
MOE_NSUB = 8
MOE_TN = 256
MOE_JGU = D_FF // MOE_TN
MOE_JD = D_MODEL // MOE_TN
N_ASSIGN = T_ALL * TOP_K
MOE_ROW_TILES = -(-(N_ASSIGN + N_EXPERTS * (MOE_SUB - 1)) // MOE_SUB)
MOE_ROWS = MOE_ROW_TILES * MOE_SUB
MOE_ITEMS = N_EXPERTS + MOE_ROW_TILES // MOE_NSUB

VMEM_LIMIT = 56 * 1024 * 1024


def _cparams(sem, vmem=VMEM_LIMIT):
    return pltpu.CompilerParams(dimension_semantics=sem, vmem_limit_bytes=vmem)


def _norm_mod(x, g, shift, scale):
    ms = jnp.mean(x * x, axis=-1, keepdims=True)
    y = x * lax.rsqrt(ms + NORM_EPS)
    return (y * g) * (1.0 + scale) + shift


def _silu(x):
    return x * jax.nn.sigmoid(x)


def _bdot(a, b):
    return jnp.dot(a, b, preferred_element_type=F32)


def _adaln_body(cs_ref, cp_ref, w_ref, b_ref, os_ref, op_ref):
    w = w_ref[0].astype(BF16)
    b = b_ref[0]
    os_ref[0] = _bdot(_silu(cs_ref[...]).astype(BF16), w) + b
    op_ref[0] = _bdot(_silu(cp_ref[...]).astype(BF16), w) + b


def _adaln(c_sample, c_prompt8, w_mod, b_mod):
    nl, d, n6 = w_mod.shape
    tn = 1024
    return pl.pallas_call(
        _adaln_body,
        out_shape=(jax.ShapeDtypeStruct((nl, DEC_BATCH, n6), F32),
                   jax.ShapeDtypeStruct((nl, 8, n6), F32)),
        grid=(nl, n6 // tn),
        in_specs=[pl.BlockSpec((DEC_BATCH, d), lambda l, n: (0, 0)),
                  pl.BlockSpec((8, d), lambda l, n: (0, 0)),
                  pl.BlockSpec((1, d, tn), lambda l, n: (l, 0, n)),
                  pl.BlockSpec((1, 1, tn), lambda l, n: (l, 0, n))],
        out_specs=(pl.BlockSpec((1, DEC_BATCH, tn), lambda l, n: (l, 0, n)),
                   pl.BlockSpec((1, 8, tn), lambda l, n: (l, 0, n))),
        compiler_params=_cparams(("arbitrary", "arbitrary")),
        name="adaln",
    )(c_sample, c_prompt8, w_mod, b_mod.reshape(nl, 1, n6))


def _pick_mod(i, mp_ref, ms_ref):
    b = jnp.minimum(i // TILES_PER_SEQ, BATCH - 1)
    return jnp.where(i == N_PROMPT_TILES, ms_ref[0], mp_ref[0, pl.ds(b, 1), :])


def _pool_group_dots(diff, pw_ref):
    ys = []
    for gi in range(len(POOL_WINDOWS)):
        cols = slice(gi * POOL_GROUP_DIM, (gi + 1) * POOL_GROUP_DIM)
        ys.append(_bdot(diff[gi] if isinstance(diff, (list, tuple)) else diff[:, cols],
                        pw_ref[0, gi].astype(BF16)))
    return jnp.concatenate(ys, axis=1)


def _pool_prompt_body(x_ref, sh_ref, sc_ref, gt_ref, g_ref, pw_ref, ps_ref, xo_ref, np_ref, hbuf, *, ts):
    b = pl.program_id(0)
    s = pl.program_id(1)

    @pl.when(s == 0)
    def _():
        hbuf[0:POOL_HALO, :] = jnp.zeros((POOL_HALO, D_MODEL), F32)

    x = x_ref[...]
    shift = sh_ref[0, pl.ds(b, 1), :]
    scale = sc_ref[0, pl.ds(b, 1), :]
    gate = gt_ref[0, pl.ds(b, 1), :]
    h = _norm_mod(x, g_ref[...], shift, scale)
    hbuf[POOL_HALO:, :] = h
    t = s * ts + lax.broadcasted_iota(I32, (ts, 1), 0)
    diffs = []
    for gi, w in enumerate(POOL_WINDOWS):
        cols = slice(gi * POOL_GROUP_DIM, (gi + 1) * POOL_GROUP_DIM)
        acc = hbuf[:, cols]
        k = 1
        while k < w:
            acc = acc + pltpu.roll(acc, k, 0)
            k *= 2
        cnt = jnp.minimum(t + 1, w).astype(F32)
        diffs.append((acc[POOL_HALO:, :] / cnt - h[:, cols]).astype(BF16))
    y = _pool_group_dots(diffs, pw_ref) * ps_ref[...]
    xo_ref[...] = x + gate * y

    @pl.when(s == pl.num_programs(1) - 1)
    def _():
        np_ref[0] = hbuf[pl.ds(POOL_HALO + ts - POOL_BUF, POOL_BUF), :]

    hbuf[0:POOL_HALO, :] = hbuf[ts:ts + POOL_HALO, :]


def _pool_prompt(x2d, row_tile0, mod_p, layer, g, pool_w, pool_scale, j):
    ts = 256
    ns = SEQ // ts
    mspec = lambda k: pl.BlockSpec((1, 8, D_MODEL), lambda b, s: (layer, 0, k))
    return pl.pallas_call(
        functools.partial(_pool_prompt_body, ts=ts),
        out_shape=(jax.ShapeDtypeStruct((TP, D_MODEL), F32),
                   jax.ShapeDtypeStruct((BATCH, POOL_BUF, D_MODEL), F32)),
        grid=(BATCH, ns),
        in_specs=[pl.BlockSpec((ts, D_MODEL), lambda b, s: (row_tile0 + b * ns + s, 0)),
                  mspec(0), mspec(1), mspec(2),
                  pl.BlockSpec((1, D_MODEL), lambda b, s: (0, 0)),
                  pl.BlockSpec((1, len(POOL_WINDOWS), POOL_GROUP_DIM, POOL_GROUP_DIM),
                               lambda b, s: (j, 0, 0, 0)),
                  pl.BlockSpec((1, D_MODEL), lambda b, s: (0, 0))],
        out_specs=(pl.BlockSpec((ts, D_MODEL), lambda b, s: (b * ns + s, 0)),
                   pl.BlockSpec((1, POOL_BUF, D_MODEL), lambda b, s: (b, 0, 0))),
        scratch_shapes=[pltpu.VMEM((POOL_HALO + ts, D_MODEL), F32)],
        compiler_params=_cparams(("arbitrary", "arbitrary")),
        name="pool_prompt",
    )(x2d, mod_p, mod_p, mod_p, g, pool_w, pool_scale)


def _pool_sample_body(x_ref, st_ref, sh_ref, sc_ref, gt_ref, g_ref, pw_ref, ps_ref, xo_ref, npo_ref, dbuf, *, sb):
    x = x_ref[...]
    h = _norm_mod(x, g_ref[...], sh_ref[0], sc_ref[0])
    row = lax.broadcasted_iota(I32, (POOL_BUF, 1), 0)
    for s in range(sb):
        st = st_ref[0, s]
        hs = h[s:s + 1, :]
        parts = []
        for gi, w in enumerate(POOL_WINDOWS):
            cols = slice(gi * POOL_GROUP_DIM, (gi + 1) * POOL_GROUP_DIM)
            tail = jnp.sum(jnp.where(row >= POOL_BUF - (w - 1), st[:, cols], 0.0), axis=0, keepdims=True)
            parts.append((tail + hs[:, cols]) / float(w) - hs[:, cols])
        dbuf[s:s + 1, :] = jnp.concatenate(parts, axis=1)
        npo_ref[s, 0:POOL_BUF - 1, :] = st[1:POOL_BUF, :]
        npo_ref[s, POOL_BUF - 1:POOL_BUF, :] = hs
    y = _pool_group_dots(dbuf[...].astype(BF16), pw_ref) * ps_ref[...]
    xo_ref[...] = x + gt_ref[0] * y


def _pool_sample(x2d, row_tile0, state_pool, mod_s, layer, g, pool_w, pool_scale, j):
    sb = 16
    mspec = lambda k: pl.BlockSpec((1, sb, D_MODEL), lambda i: (layer, i, k))
    return pl.pallas_call(
        functools.partial(_pool_sample_body, sb=sb),
        out_shape=(jax.ShapeDtypeStruct((DEC_BATCH, D_MODEL), F32),
                   jax.ShapeDtypeStruct((DEC_BATCH, POOL_BUF, D_MODEL), F32)),
        grid=(DEC_BATCH // sb,),
        in_specs=[pl.BlockSpec((sb, D_MODEL), lambda i: (row_tile0 + i, 0)),
                  pl.BlockSpec((1, sb, POOL_BUF, D_MODEL), lambda i: (j, i, 0, 0)),
                  mspec(0), mspec(1), mspec(2),
                  pl.BlockSpec((1, D_MODEL), lambda i: (0, 0)),
                  pl.BlockSpec((1, len(POOL_WINDOWS), POOL_GROUP_DIM, POOL_GROUP_DIM),
                               lambda i: (j, 0, 0, 0)),
                  pl.BlockSpec((1, D_MODEL), lambda i: (0, 0))],
        out_specs=(pl.BlockSpec((sb, D_MODEL), lambda i: (i, 0)),
                   pl.BlockSpec((sb, POOL_BUF, D_MODEL), lambda i: (i, 0, 0))),
        scratch_shapes=[pltpu.VMEM((sb, D_MODEL), F32)],
        compiler_params=_cparams(("arbitrary",)),
        name="pool_sample",
    )(x2d, state_pool, mod_s, mod_s, mod_s, g, pool_w, pool_scale)


def _norm_mod_body(x_ref, shp, scp, shs, scs, g_ref, o_ref):
    i = pl.program_id(0)
    h = _norm_mod(x_ref[...], g_ref[...], _pick_mod(i, shp, shs), _pick_mod(i, scp, scs))
    o_ref[...] = h.astype(BF16)


def _norm_mod_all(x_all, mod_p, mod_s, layer, g):
    return pl.pallas_call(
        _norm_mod_body,
        out_shape=jax.ShapeDtypeStruct((T_ALL, D_MODEL), BF16),
        grid=(N_TOK_TILES,),
        in_specs=[pl.BlockSpec((TOK_TILE, D_MODEL), lambda i: (i, 0)),
                  pl.BlockSpec((1, 8, D_MODEL), lambda i: (layer, 0, 0)),
                  pl.BlockSpec((1, 8, D_MODEL), lambda i: (layer, 0, 1)),
                  pl.BlockSpec((1, DEC_BATCH, D_MODEL), lambda i: (layer, 0, 0)),
                  pl.BlockSpec((1, DEC_BATCH, D_MODEL), lambda i: (layer, 0, 1)),
                  pl.BlockSpec((1, D_MODEL), lambda i: (0, 0))],
        out_specs=pl.BlockSpec((TOK_TILE, D_MODEL), lambda i: (i, 0)),
        compiler_params=_cparams(("arbitrary",)),
        name="ret_norm_mod",
    )(x_all, mod_p, mod_p, mod_s, mod_s, g)


def _in_proj_body(a_ref, w_ref, o_ref):
    o_ref[...] = _bdot(a_ref[...], w_ref[0].astype(BF16))


def _in_proj(h_all, w_in, j):
    tm = T_ALL // 4
    tn = 512
    return pl.pallas_call(
        _in_proj_body,
        out_shape=jax.ShapeDtypeStruct((T_ALL, RET_IN_DIM), F32),
        grid=(T_ALL // tm, RET_IN_DIM // tn),
        in_specs=[pl.BlockSpec((tm, D_MODEL), lambda m, n: (m, 0)),
                  pl.BlockSpec((1, D_MODEL, tn), lambda m, n: (j, 0, n))],
        out_specs=pl.BlockSpec((tm, tn), lambda m, n: (m, n)),
        compiler_params=_cparams(("arbitrary", "arbitrary")),
        name="ret_in_proj",
    )(h_all, w_in)


def _rope(x, cos, sin, scale=None):
    half = RET_DK // 2
    x1 = x[:, :half]
    x2 = x[:, half:]
    out = jnp.concatenate([x1 * cos - x2 * sin, x1 * sin + x2 * cos], axis=1)
    return out if scale is None else out * scale


def _log_decay(h, shape):
    hf = jnp.full(shape, h, I32).astype(F32)
    return jnp.log(1.0 - jnp.exp2(-5.0 - hf))


def _group_norm_gate(o, g):
    o = o * lax.rsqrt(jnp.mean(o * o, axis=-1, keepdims=True) + NORM_EPS)
    return _silu(g) * o


def _ret_prompt_body(q_ref, k_ref, v_ref, g_ref, cos_ref, sin_ref, u_ref, s_ref):
    h = pl.program_id(1)
    c = pl.program_id(2)
    cn = RET_CHUNK

    @pl.when(c == 0)
    def _():
        s_ref[...] = jnp.zeros_like(s_ref)

    cos = cos_ref[...]
    sin = sin_ref[...]
    q = _rope(q_ref[...], cos, sin)
    k = _rope(k_ref[...], cos, sin, RET_DK ** -0.5)
    v = v_ref[...].astype(BF16)

    i_col = lax.broadcasted_iota(I32, (cn, 1), 0).astype(F32)
    j_row = lax.broadcasted_iota(I32, (1, cn), 1).astype(F32)
    rel = i_col - j_row
    intra = jnp.where(rel >= 0, jnp.exp(_log_decay(h, (cn, cn)) * jnp.maximum(rel, 0.0)), 0.0)
    lg_col = _log_decay(h, (cn, 1))
    q_dec = jnp.exp(lg_col * (i_col + 1.0))
    k_dec = jnp.exp(lg_col * (cn - 1.0 - i_col))
    c_dec = jnp.exp(_log_decay(h, (1, 1)) * float(cn))

    state = s_ref[0, 0]
    s = lax.dot_general(q.astype(BF16), k.astype(BF16), (((1,), (1,)), ((), ())),
                        preferred_element_type=F32) * intra
    o = _bdot(s.astype(BF16), v) + _bdot((q * q_dec).astype(BF16), state.astype(BF16))
    kd_t = (k * k_dec).T.astype(BF16)
    s_ref[0, 0] = c_dec * state + _bdot(kd_t, v)
    u_ref[...] = _group_norm_gate(o, g_ref[...]).astype(BF16)


def _ret_prompt(proj, cos, sin):
    nc = SEQ // RET_CHUNK
    cn = RET_CHUNK
    row = lambda b, h, c: b * nc + c
    return pl.pallas_call(
        _ret_prompt_body,
        out_shape=(jax.ShapeDtypeStruct((TP, RET_HV), BF16),
                   jax.ShapeDtypeStruct((BATCH, RET_HEADS, RET_DK, RET_DV), F32)),
        grid=(BATCH, RET_HEADS, nc),
        in_specs=[pl.BlockSpec((cn, RET_DK), lambda b, h, c: (row(b, h, c), h)),
                  pl.BlockSpec((cn, RET_DK), lambda b, h, c: (row(b, h, c), RET_HEADS + h)),
                  pl.BlockSpec((cn, RET_DV), lambda b, h, c: (row(b, h, c), RET_HEADS + h)),
                  pl.BlockSpec((cn, RET_DV), lambda b, h, c: (row(b, h, c), 2 * RET_HEADS + h)),
                  pl.BlockSpec((cn, RET_DK // 2), lambda b, h, c: (c, 0)),
                  pl.BlockSpec((cn, RET_DK // 2), lambda b, h, c: (c, 0))],
        out_specs=(pl.BlockSpec((cn, RET_DV), lambda b, h, c: (row(b, h, c), h)),
                   pl.BlockSpec((1, 1, RET_DK, RET_DV), lambda b, h, c: (b, h, 0, 0))),
        compiler_params=_cparams(("arbitrary", "arbitrary", "arbitrary")),
        name="ret_prompt",
    )(proj, proj, proj, proj, cos, sin)


def _ret_sample_body(q_ref, k_ref, v_ref, g_ref, cos_ref, sin_ref, si_ref, *rest, sb, aliased):
    if aliased:
        rest = rest[1:]
    u_ref, so_ref, obuf = rest
    h = pl.program_id(1)
    cos = cos_ref[...]
    sin = sin_ref[...]
    gamma = jnp.exp(_log_decay(h, (1, 1)))
    q = _rope(q_ref[...], cos, sin)
    k = _rope(k_ref[...], cos, sin, RET_DK ** -0.5)
    v = v_ref[...]
    qk = jnp.sum(q * k, axis=1, keepdims=True)
    pad = jnp.zeros((LANES - sb, RET_DK), F32)
    q_t = jnp.concatenate([q * gamma, pad], axis=0).T
    k_t = jnp.concatenate([k, pad], axis=0).T
    for s in range(sb):
        state = si_ref[0, s, 0]
        vs = v[s:s + 1, :]
        o_state = jnp.sum(q_t[:, s:s + 1] * state, axis=0, keepdims=True)
        so_ref[0, s, 0] = gamma * state + k_t[:, s:s + 1] * vs
        obuf[s:s + 1, :] = qk[s:s + 1, :] * vs + o_state
    u_ref[...] = _group_norm_gate(obuf[...], g_ref[...])


def _ret_sample(proj, cos_s, sin_s, state_ret, j, prev_out):
    sb = 8
    r0 = TP // sb
    aliased = prev_out is not None
    in_specs = [pl.BlockSpec((sb, RET_DK), lambda i, h: (r0 + i, h)),
                pl.BlockSpec((sb, RET_DK), lambda i, h: (r0 + i, RET_HEADS + h)),
                pl.BlockSpec((sb, RET_DV), lambda i, h: (r0 + i, RET_HEADS + h)),
                pl.BlockSpec((sb, RET_DV), lambda i, h: (r0 + i, 2 * RET_HEADS + h)),
                pl.BlockSpec((1, RET_DK // 2), lambda i, h: (0, 0)),
                pl.BlockSpec((1, RET_DK // 2), lambda i, h: (0, 0)),
                pl.BlockSpec((1, sb, 1, RET_DK, RET_DV), lambda i, h: (j, i, h, 0, 0))]
    args = [proj, proj, proj, proj, cos_s, sin_s, state_ret]
    aliases = {}
    if aliased:
        in_specs.append(pl.BlockSpec(memory_space=pl.ANY))
        args.append(prev_out)
        aliases = {len(args) - 1: 1}
    return pl.pallas_call(
        functools.partial(_ret_sample_body, sb=sb, aliased=aliased),
        out_shape=(jax.ShapeDtypeStruct((DEC_BATCH, RET_HV), F32),
                   jax.ShapeDtypeStruct(state_ret.shape, F32)),
        grid=(DEC_BATCH // sb, RET_HEADS),
        in_specs=in_specs,
        out_specs=(pl.BlockSpec((sb, RET_DV), lambda i, h: (i, h)),
                   pl.BlockSpec((1, sb, 1, RET_DK, RET_DV), lambda i, h: (j, i, h, 0, 0))),
        scratch_shapes=[pltpu.VMEM((sb, RET_DV), F32)],
        input_output_aliases=aliases,
        compiler_params=_cparams(("arbitrary", "arbitrary")),
        name="ret_sample",
    )(*args)


def _out_proj_body(u_ref, w_ref, x_ref, gt_ref, o_ref, *, per_row, tm):
    y = _bdot(u_ref[...].astype(BF16), w_ref[0].astype(BF16))
    if per_row:
        gate = gt_ref[0]
    else:
        gate = gt_ref[0, pl.ds(pl.program_id(0) * tm // SEQ, 1), :]
    o_ref[...] = x_ref[...] + gate * y


def _out_proj(u, w_out, j, x_all, row_tile0, mod, layer, per_row):
    m = u.shape[0]
    tm = min(m, 1024)
    tn = 512
    gate_blk = 2 * (D_MODEL // tn)
    mrows = mod.shape[1]
    return pl.pallas_call(
        functools.partial(_out_proj_body, per_row=per_row, tm=tm),
        out_shape=jax.ShapeDtypeStruct((m, D_MODEL), F32),
        grid=(m // tm, D_MODEL // tn),
        in_specs=[pl.BlockSpec((tm, RET_HV), lambda i, n: (i, 0)),
                  pl.BlockSpec((1, RET_HV, tn), lambda i, n: (j, 0, n)),
                  pl.BlockSpec((tm, tn), lambda i, n: (row_tile0 + i, n)),
                  pl.BlockSpec((1, mrows, tn), lambda i, n: (layer, 0, gate_blk + n))],
        out_specs=pl.BlockSpec((tm, tn), lambda i, n: (i, n)),
        compiler_params=_cparams(("arbitrary", "arbitrary")),
        name="ret_out_proj",
    )(u, w_out, x_all, mod)


def _ffn_pre_body(xp_ref, xs_ref, shp, scp, shs, scs, g_ref, wr_ref, br_ref,
                  hw_ref, idx_ref, gate_ref, rank_ref, cnt_ref, carry):
    i = pl.program_id(0)
    tm = TOK_TILE

    @pl.when(i == 0)
    def _():
        carry[...] = jnp.zeros_like(carry)

    x = jnp.where(i == N_PROMPT_TILES, xs_ref[...], xp_ref[...])
    h = _norm_mod(x, g_ref[...], _pick_mod(i, shp, shs), _pick_mod(i, scp, scs))

    bits = lax.bitcast_convert_type(h.astype(BF16).astype(F32), U32)
    half = D_MODEL // 2
    hw_ref[...] = (bits[:, :half] >> 16) | (bits[:, half:] & jnp.uint32(0xFFFF0000))

    lane = lax.broadcasted_iota(I32, (tm, LANES), 1)
    lane_f = lane.astype(F32)
    logits = jnp.dot(h, wr_ref[0], preferred_element_type=F32, precision=lax.Precision.HIGHEST) + br_ref[0]
    logits = jnp.where(lane < N_EXPERTS, logits, -jnp.inf)

    hots, vals = [], []
    idx_out = jnp.zeros((tm, LANES), F32)
    for kk in range(TOP_K):
        m = jnp.max(logits, axis=1, keepdims=True)
        pick = jnp.min(jnp.where(logits == m, lane_f, float(LANES)), axis=1, keepdims=True)
        hot = lane_f == pick
        logits = jnp.where(hot, -jnp.inf, logits)
        hots.append(hot)
        vals.append(m)
        idx_out = jnp.where(lane == kk, pick, idx_out)
    exps = [jnp.exp(v - vals[0]) for v in vals]
    den = exps[0] + exps[1] + exps[2] + exps[3]
    gate_out = jnp.zeros((tm, LANES), F32)
    for kk in range(TOP_K):
        gate_out = jnp.where(lane == kk, exps[kk] / den, gate_out)

    cnt = jnp.zeros((tm, LANES), F32)
    for hot in hots:
        cnt = cnt + hot.astype(F32)
    r_i = lax.broadcasted_iota(I32, (tm, tm), 0)
    c_i = lax.broadcasted_iota(I32, (tm, tm), 1)
    lower = jnp.where(c_i < r_i, 1.0, 0.0).astype(BF16)
    base = _bdot(lower, cnt.astype(BF16)) + carry[0:1, :]
    rank_out = jnp.zeros((tm, LANES), F32)
    for kk in range(TOP_K):
        rk = jnp.sum(jnp.where(hots[kk], base, 0.0), axis=1, keepdims=True)
        rank_out = jnp.where(lane == kk, rk, rank_out)
    total = carry[0:1, :] + jnp.sum(cnt, axis=0, keepdims=True)
    carry[0:1, :] = total

    idx_ref[...] = idx_out.astype(I32)
    gate_ref[...] = gate_out
    rank_ref[...] = rank_out.astype(I32)

    @pl.when(i == pl.num_programs(0) - 1)
    def _():
        cnt_ref[...] = jnp.broadcast_to(total, cnt_ref.shape).astype(I32)


def _ffn_pre(x_p, x_s, mod_p, mod_s, layer, g, w_router_pad, b_router_pad):
    tm = TOK_TILE
    tok = lambda i: (i, 0)
    return pl.pallas_call(
        _ffn_pre_body,
        out_shape=(jax.ShapeDtypeStruct((T_ALL, D_MODEL // 2), U32),
                   jax.ShapeDtypeStruct((T_ALL, LANES), I32),
                   jax.ShapeDtypeStruct((T_ALL, LANES), F32),
                   jax.ShapeDtypeStruct((T_ALL, LANES), I32),
                   jax.ShapeDtypeStruct((8, LANES), I32)),
        grid=(N_TOK_TILES,),
        in_specs=[pl.BlockSpec((tm, D_MODEL), lambda i: (jnp.minimum(i, N_PROMPT_TILES - 1), 0)),
                  pl.BlockSpec((tm, D_MODEL), lambda i: (0, 0)),
                  pl.BlockSpec((1, 8, D_MODEL), lambda i: (layer, 0, 3)),
                  pl.BlockSpec((1, 8, D_MODEL), lambda i: (layer, 0, 4)),
                  pl.BlockSpec((1, DEC_BATCH, D_MODEL), lambda i: (layer, 0, 3)),
                  pl.BlockSpec((1, DEC_BATCH, D_MODEL), lambda i: (layer, 0, 4)),
                  pl.BlockSpec((1, D_MODEL), lambda i: (0, 0)),
                  pl.BlockSpec((1, D_MODEL, LANES), lambda i: (layer, 0, 0)),
                  pl.BlockSpec((1, 1, LANES), lambda i: (layer, 0, 0))],
        out_specs=(pl.BlockSpec((tm, D_MODEL // 2), tok),
                   pl.BlockSpec((tm, LANES), tok),
                   pl.BlockSpec((tm, LANES), tok),
                   pl.BlockSpec((tm, LANES), tok),
                   pl.BlockSpec((8, LANES), lambda i: (0, 0))),
        scratch_shapes=[pltpu.VMEM((8, LANES), F32)],
        compiler_params=_cparams(("arbitrary",)),
        name="ffn_pre",
    )(x_p, x_s, mod_p, mod_p, mod_s, mod_s, g, w_router_pad, b_router_pad)


def _route(idx, rank, counts):
    e = idx[:, :TOP_K]
    r = rank[:, :TOP_K]
    cnt = counts[0, :N_EXPERTS]
    ntile = (cnt + MOE_SUB - 1) // MOE_SUB
    tile_end = jnp.cumsum(ntile)
    tile_start = tile_end - ntile
    pos = (tile_start[e] * MOE_SUB + r).astype(I32)
    tok = jnp.broadcast_to(jnp.arange(T_ALL, dtype=I32)[:, None], pos.shape)
    row_tok = jnp.zeros((MOE_ROWS,), I32).at[pos.reshape(-1)].set(tok.reshape(-1))
    npass = (ntile + MOE_NSUB - 1) // MOE_NSUB
    item_end = jnp.cumsum(npass)
    item_start = item_end - npass
    n_items = item_end[-1]
    it = jnp.arange(MOE_ITEMS, dtype=I32)
    it_c = jnp.minimum(it, n_items - 1)
    item_e = jnp.minimum(jnp.searchsorted(item_end, it_c, side="right"), N_EXPERTS - 1).astype(I32)
    p = it_c - item_start[item_e]
    item_row0 = ((tile_start[item_e] + p * MOE_NSUB) * MOE_SUB).astype(I32)
    item_nsub = jnp.where(it < n_items, jnp.minimum(ntile[item_e] - p * MOE_NSUB, MOE_NSUB), 0).astype(I32)
    return pos.reshape(-1), row_tok, item_e, item_row0, item_nsub, n_items.reshape(1).astype(I32)


def _experts_body(item_e, item_row0, item_nsub, n_items, row_tok,
                  hw_hbm, wg_ref, wu_ref, bg_ref, bu_ref, wd_ref, bd_ref,
                  ys_hbm,
                  xraw, xb, hbuf, wgb, wub, wdb, ybuf, gsem, ysem):
    it = pl.program_id(0)
    j = pl.program_id(1)
    nsub = item_nsub[it]
    row0 = item_row0[it]
    active = it < n_items[0]
    ts = MOE_SUB
    tn = MOE_TN

    def row_copy(r):
        return pltpu.make_async_copy(hw_hbm.at[row_tok[row0 + r]], xraw.at[r], gsem.at[r // ts])

    def out_copy(s, slot, jd):
        dst = ys_hbm.at[pl.ds(pl.multiple_of(row0 + s * ts, ts), ts), pl.ds(pl.multiple_of(jd * tn, tn), tn)]
        return pltpu.make_async_copy(ybuf.at[slot], dst, ysem.at[slot])

    @pl.when(jnp.logical_and(active, j == 0))
    def _gather():
        def issue(r, c):
            row_copy(r).start()
            return c
        lax.fori_loop(0, nsub * ts, issue, 0)

        def land(s, c):
            rows = pl.ds(pl.multiple_of(s * ts, ts), ts)
            pltpu.make_async_copy(hw_hbm.at[pl.ds(0, ts)], xraw.at[rows], gsem.at[s]).wait()
            w = xraw[rows, :]
            lo = lax.bitcast_convert_type(w << 16, F32).astype(BF16)
            hi = lax.bitcast_convert_type(w & jnp.uint32(0xFFFF0000), F32).astype(BF16)
            xb[rows, :] = jnp.concatenate([lo, hi], axis=1)
            return c
        lax.fori_loop(0, nsub, land, 0)

    @pl.when(jnp.logical_and(active, j < MOE_JGU))
    def _gate_up():
        wgb[...] = wg_ref[0, 0].astype(BF16)
        wub[...] = wu_ref[0, 0].astype(BF16)

        def sub(s, c):
            rows = pl.ds(pl.multiple_of(s * ts, ts), ts)
            x = xb[rows, :]
            gt = jnp.minimum(_bdot(x, wgb[...]) + bg_ref[0, 0], SWIGLU_LIMIT)
            up = jnp.clip(_bdot(x, wub[...]) + bu_ref[0, 0], -SWIGLU_LIMIT, SWIGLU_LIMIT)
            hbuf[j, rows, :] = ((up + 1.0) * gt * jax.nn.sigmoid(SWIGLU_ALPHA * gt)).astype(BF16)
            return c
        lax.fori_loop(0, nsub, sub, 0)

    @pl.when(jnp.logical_and(active, j >= MOE_JGU))
    def _down():
        jd = j - MOE_JGU
        wdb[...] = wd_ref[0, 0].astype(BF16)

        def sub(s, c):
            rows = pl.ds(pl.multiple_of(s * ts, ts), ts)
            slot = s % 2

            @pl.when(s >= 2)
            def _():
                out_copy(s - 2, slot, jd).wait()

            acc = jnp.broadcast_to(bd_ref[0, 0], (ts, tn))
            for cc in range(MOE_JGU):
                acc = acc + _bdot(hbuf[cc, rows, :], wdb[cc * tn:(cc + 1) * tn, :])
            ybuf[slot] = acc
            out_copy(s, slot, jd).start()
            return c
        lax.fori_loop(0, nsub, sub, 0)

        @pl.when(nsub >= 2)
        def _():
            out_copy(nsub - 2, nsub % 2, jd).wait()

        @pl.when(nsub >= 1)
        def _():
            out_copy(nsub - 1, (nsub - 1) % 2, jd).wait()

    @pl.when(jnp.logical_and(it == pl.num_programs(0) - 1, j == pl.num_programs(1) - 1))
    def _fill_tail():
        ybuf[0] = jnp.zeros((ts, tn), F32)
        last = n_items[0] - 1
        used = item_row0[last] // ts + item_nsub[last]

        def tail_copy(t, c):
            dst = ys_hbm.at[pl.ds(pl.multiple_of(t * ts, ts), ts), pl.ds(c * tn, tn)]
            return pltpu.make_async_copy(ybuf.at[0], dst, ysem.at[0])

        def fill(t, carry):
            for c in range(MOE_JD):
                tail_copy(t, c).start()
            for c in range(MOE_JD):
                tail_copy(t, c).wait()
            return carry
        lax.fori_loop(used, MOE_ROW_TILES, fill, 0)


def _experts(route, hw, w_gate_up, b_gate_up, w_down, b_down, layer):
    _, row_tok, item_e, item_row0, item_nsub, n_items = route
    tn = MOE_TN
    jgu = MOE_JGU
    gu_c = lambda j: jnp.minimum(j, jgu - 1)
    dn_c = lambda j: jnp.maximum(j - jgu, 0)
    grid_spec = pltpu.PrefetchScalarGridSpec(
        num_scalar_prefetch=5,
        grid=(MOE_ITEMS, MOE_JGU + MOE_JD),
        in_specs=[
            pl.BlockSpec(memory_space=pl.ANY),
            pl.BlockSpec((1, 1, D_MODEL, tn), lambda it, j, ie, *_: (layer, ie[it], 0, gu_c(j))),
            pl.BlockSpec((1, 1, D_MODEL, tn), lambda it, j, ie, *_: (layer, ie[it], 0, jgu + gu_c(j))),
            pl.BlockSpec((1, 1, 1, tn), lambda it, j, ie, *_: (layer, ie[it], 0, gu_c(j))),
            pl.BlockSpec((1, 1, 1, tn), lambda it, j, ie, *_: (layer, ie[it], 0, jgu + gu_c(j))),
            pl.BlockSpec((1, 1, D_FF, tn), lambda it, j, ie, *_: (layer, ie[it], 0, dn_c(j))),
            pl.BlockSpec((1, 1, 1, tn), lambda it, j, ie, *_: (layer, ie[it], 0, dn_c(j))),
        ],
        out_specs=pl.BlockSpec(memory_space=pl.ANY),
        scratch_shapes=[
            pltpu.VMEM((MOE_NSUB * MOE_SUB, D_MODEL // 2), U32),
            pltpu.VMEM((MOE_NSUB * MOE_SUB, D_MODEL), BF16),
            pltpu.VMEM((MOE_JGU, MOE_NSUB * MOE_SUB, tn), BF16),
            pltpu.VMEM((D_MODEL, tn), BF16),
            pltpu.VMEM((D_MODEL, tn), BF16),
            pltpu.VMEM((D_FF, tn), BF16),
            pltpu.VMEM((2, MOE_SUB, tn), F32),
            pltpu.SemaphoreType.DMA((MOE_NSUB,)),
            pltpu.SemaphoreType.DMA((2,)),
        ],
    )
    ne = w_gate_up.shape[1]
    return pl.pallas_call(
        _experts_body,
        out_shape=jax.ShapeDtypeStruct((MOE_ROWS, D_MODEL), F32),
        grid_spec=grid_spec,
        compiler_params=_cparams(("arbitrary", "arbitrary")),
        name="moe_experts",
    )(item_e, item_row0, item_nsub, n_items, row_tok,
      hw, w_gate_up, w_gate_up,
      b_gate_up.reshape(DEPTH, ne, 1, 2 * D_FF), b_gate_up.reshape(DEPTH, ne, 1, 2 * D_FF),
      w_down, b_down.reshape(DEPTH, ne, 1, D_MODEL))


def _combine_body(pos, ys_hbm, xp_ref, xs_ref, gate_ref, gtp, gts, o_ref, ybuf, sem):
    i = pl.program_id(0)
    tm = TOK_TILE
    base = i * (tm * TOP_K)

    def issue(t, c):
        for kk in range(TOP_K):
            pltpu.make_async_copy(ys_hbm.at[pos[base + t * TOP_K + kk]], ybuf.at[kk, t], sem).start()
        return c
    lax.fori_loop(0, tm, issue, 0)
    for kk in range(TOP_K):
        pltpu.make_async_copy(ys_hbm.at[pl.ds(0, tm)], ybuf.at[kk], sem).wait()

    gates = gate_ref[...]
    f = gates[:, 0:1] * ybuf[0]
    for kk in range(1, TOP_K):
        f = f + gates[:, kk:kk + 1] * ybuf[kk]
    x = jnp.where(i == N_PROMPT_TILES, xs_ref[...], xp_ref[...])
    o_ref[...] = x + _pick_mod(i, gtp, gts) * f


def _combine(pos, ys, x_p, x_s, gates, mod_p, mod_s, layer):
    tm = TOK_TILE
    grid_spec = pltpu.PrefetchScalarGridSpec(
        num_scalar_prefetch=1,
        grid=(N_TOK_TILES,),
        in_specs=[pl.BlockSpec(memory_space=pl.ANY),
                  pl.BlockSpec((tm, D_MODEL), lambda i, *_: (jnp.minimum(i, N_PROMPT_TILES - 1), 0)),
                  pl.BlockSpec((tm, D_MODEL), lambda i, *_: (0, 0)),
                  pl.BlockSpec((tm, LANES), lambda i, *_: (i, 0)),
                  pl.BlockSpec((1, 8, D_MODEL), lambda i, *_: (layer, 0, 5)),
                  pl.BlockSpec((1, DEC_BATCH, D_MODEL), lambda i, *_: (layer, 0, 5))],
        out_specs=pl.BlockSpec((tm, D_MODEL), lambda i, *_: (i, 0)),
        scratch_shapes=[pltpu.VMEM((TOP_K, tm, D_MODEL), F32), pltpu.SemaphoreType.DMA],
    )
    return pl.pallas_call(
        _combine_body,
        out_shape=jax.ShapeDtypeStruct((T_ALL, D_MODEL), F32),
        grid_spec=grid_spec,
        compiler_params=_cparams(("arbitrary",)),
        name="moe_combine",
    )(pos, ys, x_p, x_s, gates, mod_p, mod_s)


def _final_norm_body(x_ref, g_ref, op_ref, os_ref):
    i = pl.program_id(0)
    x = x_ref[...]
    y = x * lax.rsqrt(jnp.mean(x * x, axis=-1, keepdims=True) + NORM_EPS) * g_ref[...]

    @pl.when(i < N_PROMPT_TILES)
    def _():
        op_ref[...] = y

    @pl.when(i == N_PROMPT_TILES)
    def _():
        os_ref[...] = y


def _final_norm(x_all, g):
    tm = TOK_TILE
    return pl.pallas_call(
        _final_norm_body,
        out_shape=(jax.ShapeDtypeStruct((TP, D_MODEL), F32),
                   jax.ShapeDtypeStruct((DEC_BATCH, D_MODEL), F32)),
        grid=(N_TOK_TILES,),
        in_specs=[pl.BlockSpec((tm, D_MODEL), lambda i: (i, 0)),
                  pl.BlockSpec((1, D_MODEL), lambda i: (0, 0))],
        out_specs=(pl.BlockSpec((tm, D_MODEL), lambda i: (jnp.minimum(i, N_PROMPT_TILES - 1), 0)),
                   pl.BlockSpec((tm, D_MODEL), lambda i: (0, 0))),
        compiler_params=_cparams(("arbitrary",)),
        name="final_norm",
    )(x_all, g)


def _rope_tables(pos):
    inv = ROPE_BASE ** (-jnp.linspace(0.0, 1.0, RET_DK // 2, dtype=F32))
    ang = pos.astype(F32)[:, None] * inv[None, :]
    return jnp.cos(ang), jnp.sin(ang)


def kernel(x_prompt, x_sample, state_pool, state_ret, c_prompt, c_sample, w_mod, b_mod, g_norm_mix, g_norm_ffn,
           pool_w, pool_scale, ret_w_in, ret_w_out, moe_w_router, moe_b_router, moe_w_gate_up, moe_b_gate_up,
           moe_w_down, moe_b_down, g_final):
    assert x_prompt.shape == (BATCH, SEQ, D_MODEL) and x_sample.shape == (DEC_BATCH, 1, D_MODEL)
    c_prompt8 = jnp.concatenate([c_prompt, jnp.zeros((8 - BATCH, D_MODEL), F32)], axis=0)
    mod_s, mod_p = _adaln(c_sample, c_prompt8, w_mod, b_mod)
    cos_p, sin_p = _rope_tables(jnp.arange(SEQ))
    cos_s, sin_s = _rope_tables(PAST_LEN + jnp.arange(1))
    w_router_pad = jnp.pad(moe_w_router, ((0, 0), (0, 0), (0, LANES - N_EXPERTS)))
    b_router_pad = jnp.pad(moe_b_router, ((0, 0), (0, LANES - N_EXPERTS))).reshape(DEPTH, 1, LANES)

    x_p_src, x_p_tile0 = x_prompt.reshape(TP, D_MODEL), 0
    x_s_src, x_s_rows0 = x_sample.reshape(DEC_BATCH, D_MODEL), 0
    x_all = None
    new_pool_p, new_pool_s, new_ret_p = [], [], []
    ret_s_out = None
    for layer in range(DEPTH):
        j = layer // N_MIXERS
        g_mix = g_norm_mix[layer].reshape(1, D_MODEL)
        g_ffn = g_norm_ffn[layer].reshape(1, D_MODEL)
        if layer % N_MIXERS == 0:
            scale = pool_scale[j].reshape(1, D_MODEL)
            x_p, pool_p = _pool_prompt(x_p_src, x_p_tile0, mod_p, layer, g_mix, pool_w, scale, j)
            x_s, pool_s = _pool_sample(x_s_src, x_s_rows0 // 16, state_pool, mod_s, layer, g_mix, pool_w, scale, j)
            new_pool_p.append(pool_p)
            new_pool_s.append(pool_s)
        else:
            h_all = _norm_mod_all(x_all, mod_p, mod_s, layer, g_mix)
            proj = _in_proj(h_all, ret_w_in, j)
            u_p, ret_p = _ret_prompt(proj, cos_p, sin_p)
            u_s, ret_s_out = _ret_sample(proj, cos_s, sin_s, state_ret, j, ret_s_out)
            new_ret_p.append(ret_p)
            x_p = _out_proj(u_p, ret_w_out, j, x_all, 0, mod_p, layer, per_row=False)
            x_s = _out_proj(u_s, ret_w_out, j, x_all, TP // DEC_BATCH, mod_s, layer, per_row=True)
        hw, idx, gates, rank, counts = _ffn_pre(x_p, x_s, mod_p, mod_s, layer, g_ffn, w_router_pad, b_router_pad)
        route = _route(idx, rank, counts)
        ys = _experts(route, hw, moe_w_gate_up, moe_b_gate_up, moe_w_down, moe_b_down, layer)
        x_all = _combine(route[0], ys, x_p, x_s, gates, mod_p, mod_s, layer)
        x_p_src, x_p_tile0 = x_all, 0
        x_s_src, x_s_rows0 = x_all, TP
    y_p, y_s = _final_norm(x_all, g_final.reshape(1, D_MODEL))
    return (y_p.reshape(BATCH, SEQ, D_MODEL), y_s.reshape(DEC_BATCH, 1, D_MODEL),
            jnp.stack(new_pool_p), jnp.stack(new_pool_s), jnp.stack(new_ret_p), ret_s_out)
```

```python
import functools

import jax
import jax.numpy as jnp
import numpy as np
from jax import lax
from jax.experimental import pallas as pl
from jax.experimental.pallas import tpu as pltpu

F32 = jnp.float32
BF16 = jnp.bfloat16
I32 = jnp.int32
U32 = jnp.uint32

D_MODEL = 2048
BATCH = 4
SEQ = 2048
DEPTH = 4
DEC_BATCH = 128
PAST_LEN = 16384
N_MIXERS = 2
POOL_WINDOWS = (2, 4, 8, 16)
POOL_GROUP_DIM = D_MODEL // len(POOL_WINDOWS)
POOL_BUF = max(POOL_WINDOWS) - 1
POOL_HALO = 16
RET_HEADS = 8
RET_DK = D_MODEL // RET_HEADS
RET_DV = 2 * D_MODEL // RET_HEADS
RET_CHUNK = 128
RET_HK = RET_HEADS * RET_DK
RET_HV = RET_HEADS * RET_DV
RET_IN_DIM = 2 * RET_HK + 2 * RET_HV
ROPE_BASE = 10000.0
N_EXPERTS = 32
TOP_K = 4
D_FF = D_MODEL
SWIGLU_LIMIT = 7.0
SWIGLU_ALPHA = 1.702
NORM_EPS = 1e-5

TP = BATCH * SEQ
T_ALL = TP + DEC_BATCH
TOK_TILE = 128
N_TOK_TILES = T_ALL // TOK_TILE
N_PROMPT_TILES = TP // TOK_TILE
TILES_PER_SEQ = SEQ // TOK_TILE
LANES = 128

MOE_SUB = 256
MOE_NSUB = 8
MOE_TN = 512
MOE_JGU = D_FF // MOE_TN
MOE_JD = D_MODEL // MOE_TN
N_ASSIGN = T_ALL * TOP_K
MOE_ROW_TILES = -(-(N_ASSIGN + N_EXPERTS * (MOE_SUB - 1)) // MOE_SUB)
MOE_ROWS = MOE_ROW_TILES * MOE_SUB
MOE_ITEMS = N_EXPERTS + MOE_ROW_TILES // MOE_NSUB

VMEM_LIMIT = 56 * 1024 * 1024


def _cparams(sem, vmem=VMEM_LIMIT):
    return pltpu.CompilerParams(dimension_semantics=sem, vmem_limit_bytes=vmem)


def _norm_mod(x, g, shift, scale):
    ms = jnp.mean(x * x, axis=-1, keepdims=True)
    y = x * lax.rsqrt(ms + NORM_EPS)
    return (y * g) * (1.0 + scale) + shift


def _silu(x):
    return x * jax.nn.sigmoid(x)


def _bdot(a, b):
    return jnp.dot(a, b, preferred_element_type=F32)


def _adaln_body(cs_ref, cp_ref, w_ref, b_ref, os_ref, op_ref):
    w = w_ref[0].astype(BF16)
    b = b_ref[0]
    os_ref[0] = _bdot(_silu(cs_ref[...]).astype(BF16), w) + b
    op_ref[0] = _bdot(_silu(cp_ref[...]).astype(BF16), w) + b


def _adaln(c_sample, c_prompt8, w_mod, b_mod):
    nl, d, n6 = w_mod.shape
    tn = 1024
    return pl.pallas_call(
        _adaln_body,
        out_shape=(jax.ShapeDtypeStruct((nl, DEC_BATCH, n6), F32),
                   jax.ShapeDtypeStruct((nl, 8, n6), F32)),
        grid=(nl, n6 // tn),
        in_specs=[pl.BlockSpec((DEC_BATCH, d), lambda l, n: (0, 0)),
                  pl.BlockSpec((8, d), lambda l, n: (0, 0)),
                  pl.BlockSpec((1, d, tn), lambda l, n: (l, 0, n)),
                  pl.BlockSpec((1, 1, tn), lambda l, n: (l, 0, n))],
        out_specs=(pl.BlockSpec((1, DEC_BATCH, tn), lambda l, n: (l, 0, n)),
                   pl.BlockSpec((1, 8, tn), lambda l, n: (l, 0, n))),
        compiler_params=_cparams(("arbitrary", "arbitrary")),
        name="adaln",
    )(c_sample, c_prompt8, w_mod, b_mod.reshape(nl, 1, n6))


def _pick_mod(i, mp_ref, ms_ref):
    b = jnp.minimum(i // TILES_PER_SEQ, BATCH - 1)
    return jnp.where(i == N_PROMPT_TILES, ms_ref[0], mp_ref[0, pl.ds(b, 1), :])


def _pool_group_dots(diff, pw_ref):
    ys = []
    for gi in range(len(POOL_WINDOWS)):
        cols = slice(gi * POOL_GROUP_DIM, (gi + 1) * POOL_GROUP_DIM)
        ys.append(_bdot(diff[gi] if isinstance(diff, (list, tuple)) else diff[:, cols],
                        pw_ref[0, gi].astype(BF16)))
    return jnp.concatenate(ys, axis=1)


def _pool_prompt_body(x_ref, sh_ref, sc_ref, gt_ref, g_ref, pw_ref, ps_ref, xo_ref, np_ref, hbuf, *, ts):
    b = pl.program_id(0)
    s = pl.program_id(1)

    @pl.when(s == 0)
    def _():
        hbuf[0:POOL_HALO, :] = jnp.zeros((POOL_HALO, D_MODEL), F32)

    x = x_ref[...]
    shift = sh_ref[0, pl.ds(b, 1), :]
    scale = sc_ref[0, pl.ds(b, 1), :]
    gate = gt_ref[0, pl.ds(b, 1), :]
    h = _norm_mod(x, g_ref[...], shift, scale)
    hbuf[POOL_HALO:, :] = h
    t = s * ts + lax.broadcasted_iota(I32, (ts, 1), 0)
    diffs = []
    for gi, w in enumerate(POOL_WINDOWS):
        cols = slice(gi * POOL_GROUP_DIM, (gi + 1) * POOL_GROUP_DIM)
        acc = hbuf[:, cols]
        k = 1
        while k < w:
            acc = acc + pltpu.roll(acc, k, 0)
            k *= 2
        cnt = jnp.minimum(t + 1, w).astype(F32)
        diffs.append((acc[POOL_HALO:, :] / cnt - h[:, cols]).astype(BF16))
    y = _pool_group_dots(diffs, pw_ref) * ps_ref[...]
    xo_ref[...] = x + gate * y

    @pl.when(s == pl.num_programs(1) - 1)
    def _():
        np_ref[0] = hbuf[pl.ds(POOL_HALO + ts - POOL_BUF, POOL_BUF), :]

    hbuf[0:POOL_HALO, :] = hbuf[ts:ts + POOL_HALO, :]


def _pool_prompt(x2d, row_tile0, mod_p, layer, g, pool_w, pool_scale, j):
    ts = 256
    ns = SEQ // ts
    mspec = lambda k: pl.BlockSpec((1, 8, D_MODEL), lambda b, s: (layer, 0, k))
    return pl.pallas_call(
        functools.partial(_pool_prompt_body, ts=ts),
        out_shape=(jax.ShapeDtypeStruct((TP, D_MODEL), F32),
                   jax.ShapeDtypeStruct((BATCH, POOL_BUF, D_MODEL), F32)),
        grid=(BATCH, ns),
        in_specs=[pl.BlockSpec((ts, D_MODEL), lambda b, s: (row_tile0 + b * ns + s, 0)),
                  mspec(0), mspec(1), mspec(2),
                  pl.BlockSpec((1, D_MODEL), lambda b, s: (0, 0)),
                  pl.BlockSpec((1, len(POOL_WINDOWS), POOL_GROUP_DIM, POOL_GROUP_DIM),
                               lambda b, s: (j, 0, 0, 0)),
                  pl.BlockSpec((1, D_MODEL), lambda b, s: (0, 0))],
        out_specs=(pl.BlockSpec((ts, D_MODEL), lambda b, s: (b * ns + s, 0)),
                   pl.BlockSpec((1, POOL_BUF, D_MODEL), lambda b, s: (b, 0, 0))),
        scratch_shapes=[pltpu.VMEM((POOL_HALO + ts, D_MODEL), F32)],
        compiler_params=_cparams(("arbitrary", "arbitrary")),
        name="pool_prompt",
    )(x2d, mod_p, mod_p, mod_p, g, pool_w, pool_scale)


def _pool_sample_body(x_ref, st_ref, sh_ref, sc_ref, gt_ref, g_ref, pw_ref, ps_ref, xo_ref, npo_ref, dbuf, *, sb):
    x = x_ref[...]
    h = _norm_mod(x, g_ref[...], sh_ref[0], sc_ref[0])
    row = lax.broadcasted_iota(I32, (POOL_BUF, 1), 0)
    for s in range(sb):
        st = st_ref[0, s]
        hs = h[s:s + 1, :]
        parts = []
        for gi, w in enumerate(POOL_WINDOWS):
            cols = slice(gi * POOL_GROUP_DIM, (gi + 1) * POOL_GROUP_DIM)
            tail = jnp.sum(jnp.where(row >= POOL_BUF - (w - 1), st[:, cols], 0.0), axis=0, keepdims=True)
            parts.append((tail + hs[:, cols]) / float(w) - hs[:, cols])
        dbuf[s:s + 1, :] = jnp.concatenate(parts, axis=1)
        npo_ref[s, 0:POOL_BUF - 1, :] = st[1:POOL_BUF, :]
        npo_ref[s, POOL_BUF - 1:POOL_BUF, :] = hs
    y = _pool_group_dots(dbuf[...].astype(BF16), pw_ref) * ps_ref[...]
    xo_ref[...] = x + gt_ref[0] * y


def _pool_sample(x2d, row_tile0, state_pool, mod_s, layer, g, pool_w, pool_scale, j):
    sb = 16
    mspec = lambda k: pl.BlockSpec((1, sb, D_MODEL), lambda i: (layer, i, k))
    return pl.pallas_call(
        functools.partial(_pool_sample_body, sb=sb),
        out_shape=(jax.ShapeDtypeStruct((DEC_BATCH, D_MODEL), F32),
                   jax.ShapeDtypeStruct((DEC_BATCH, POOL_BUF, D_MODEL), F32)),
        grid=(DEC_BATCH // sb,),
        in_specs=[pl.BlockSpec((sb, D_MODEL), lambda i: (row_tile0 + i, 0)),
                  pl.BlockSpec((1, sb, POOL_BUF, D_MODEL), lambda i: (j, i, 0, 0)),
                  mspec(0), mspec(1), mspec(2),
                  pl.BlockSpec((1, D_MODEL), lambda i: (0, 0)),
                  pl.BlockSpec((1, len(POOL_WINDOWS), POOL_GROUP_DIM, POOL_GROUP_DIM),
                               lambda i: (j, 0, 0, 0)),
                  pl.BlockSpec((1, D_MODEL), lambda i: (0, 0))],
        out_specs=(pl.BlockSpec((sb, D_MODEL), lambda i: (i, 0)),
                   pl.BlockSpec((sb, POOL_BUF, D_MODEL), lambda i: (i, 0, 0))),
        scratch_shapes=[pltpu.VMEM((sb, D_MODEL), F32)],
        compiler_params=_cparams(("arbitrary",)),
        name="pool_sample",
    )(x2d, state_pool, mod_s, mod_s, mod_s, g, pool_w, pool_scale)


def _norm_mod_body(x_ref, shp, scp, shs, scs, g_ref, o_ref):
    i = pl.program_id(0)
    h = _norm_mod(x_ref[...], g_ref[...], _pick_mod(i, shp, shs), _pick_mod(i, scp, scs))
    o_ref[...] = h.astype(BF16)


def _norm_mod_all(x_all, mod_p, mod_s, layer, g):
    return pl.pallas_call(
        _norm_mod_body,
        out_shape=jax.ShapeDtypeStruct((T_ALL, D_MODEL), BF16),
        grid=(N_TOK_TILES,),
        in_specs=[pl.BlockSpec((TOK_TILE, D_MODEL), lambda i: (i, 0)),
                  pl.BlockSpec((1, 8, D_MODEL), lambda i: (layer, 0, 0)),
                  pl.BlockSpec((1, 8, D_MODEL), lambda i: (layer, 0, 1)),
                  pl.BlockSpec((1, DEC_BATCH, D_MODEL), lambda i: (layer, 0, 0)),
                  pl.BlockSpec((1, DEC_BATCH, D_MODEL), lambda i: (layer, 0, 1)),
                  pl.BlockSpec((1, D_MODEL), lambda i: (0, 0))],
        out_specs=pl.BlockSpec((TOK_TILE, D_MODEL), lambda i: (i, 0)),
        compiler_params=_cparams(("arbitrary",)),
        name="ret_norm_mod",
    )(x_all, mod_p, mod_p, mod_s, mod_s, g)


def _in_proj_body(a_ref, w_ref, o_ref):
    o_ref[...] = _bdot(a_ref[...], w_ref[0].astype(BF16))


def _in_proj(h_all, w_in, j):
    tm = T_ALL // 4
    tn = 512
    return pl.pallas_call(
        _in_proj_body,
        out_shape=jax.ShapeDtypeStruct((T_ALL, RET_IN_DIM), F32),
        grid=(T_ALL // tm, RET_IN_DIM // tn),
        in_specs=[pl.BlockSpec((tm, D_MODEL), lambda m, n: (m, 0)),
                  pl.BlockSpec((1, D_MODEL, tn), lambda m, n: (j, 0, n))],
        out_specs=pl.BlockSpec((tm, tn), lambda m, n: (m, n)),
        compiler_params=_cparams(("arbitrary", "arbitrary")),
        name="ret_in_proj",
    )(h_all, w_in)


def _rope(x, cos, sin, scale=None):
    half = RET_DK // 2
    x1 = x[:, :half]
    x2 = x[:, half:]
    out = jnp.concatenate([x1 * cos - x2 * sin, x1 * sin + x2 * cos], axis=1)
    return out if scale is None else out * scale


def _log_decay(h, shape):
    hf = jnp.full(shape, h, I32).astype(F32)
    return jnp.log(1.0 - jnp.exp2(-5.0 - hf))


def _group_norm_gate(o, g):
    o = o * lax.rsqrt(jnp.mean(o * o, axis=-1, keepdims=True) + NORM_EPS)
    return _silu(g) * o


def _ret_prompt_body(q_ref, k_ref, v_ref, g_ref, cos_ref, sin_ref, u_ref, s_ref):
    h = pl.program_id(1)
    c = pl.program_id(2)
    cn = RET_CHUNK

    @pl.when(c == 0)
    def _():
        s_ref[...] = jnp.zeros_like(s_ref)

    cos = cos_ref[...]
    sin = sin_ref[...]
    q = _rope(q_ref[...], cos, sin)
    k = _rope(k_ref[...], cos, sin, RET_DK ** -0.5)
    v = v_ref[...].astype(BF16)

    i_col = lax.broadcasted_iota(I32, (cn, 1), 0).astype(F32)
    j_row = lax.broadcasted_iota(I32, (1, cn), 1).astype(F32)
    rel = i_col - j_row
    intra = jnp.where(rel >= 0, jnp.exp(_log_decay(h, (cn, cn)) * jnp.maximum(rel, 0.0)), 0.0)
    lg_col = _log_decay(h, (cn, 1))
    q_dec = jnp.exp(lg_col * (i_col + 1.0))
    k_dec = jnp.exp(lg_col * (cn - 1.0 - i_col))
    c_dec = jnp.exp(_log_decay(h, (1, 1)) * float(cn))

    state = s_ref[0, 0]
    s = lax.dot_general(q.astype(BF16), k.astype(BF16), (((1,), (1,)), ((), ())),
                        preferred_element_type=F32) * intra
    o = _bdot(s.astype(BF16), v) + _bdot((q * q_dec).astype(BF16), state.astype(BF16))
    kd_t = (k * k_dec).T.astype(BF16)
    s_ref[0, 0] = c_dec * state + _bdot(kd_t, v)
    u_ref[...] = _group_norm_gate(o, g_ref[...]).astype(BF16)


def _ret_prompt(proj, cos, sin):
    nc = SEQ // RET_CHUNK
    cn = RET_CHUNK
    row = lambda b, h, c: b * nc + c
    return pl.pallas_call(
        _ret_prompt_body,
        out_shape=(jax.ShapeDtypeStruct((TP, RET_HV), BF16),
                   jax.ShapeDtypeStruct((BATCH, RET_HEADS, RET_DK, RET_DV), F32)),
        grid=(BATCH, RET_HEADS, nc),
        in_specs=[pl.BlockSpec((cn, RET_DK), lambda b, h, c: (row(b, h, c), h)),
                  pl.BlockSpec((cn, RET_DK), lambda b, h, c: (row(b, h, c), RET_HEADS + h)),
                  pl.BlockSpec((cn, RET_DV), lambda b, h, c: (row(b, h, c), RET_HEADS + h)),
                  pl.BlockSpec((cn, RET_DV), lambda b, h, c: (row(b, h, c), 2 * RET_HEADS + h)),
                  pl.BlockSpec((cn, RET_DK // 2), lambda b, h, c: (c, 0)),
                  pl.BlockSpec((cn, RET_DK // 2), lambda b, h, c: (c, 0))],
        out_specs=(pl.BlockSpec((cn, RET_DV), lambda b, h, c: (row(b, h, c), h)),
                   pl.BlockSpec((1, 1, RET_DK, RET_DV), lambda b, h, c: (b, h, 0, 0))),
        compiler_params=_cparams(("arbitrary", "arbitrary", "arbitrary")),
        name="ret_prompt",
    )(proj, proj, proj, proj, cos, sin)


def _ret_sample_body(q_ref, k_ref, v_ref, g_ref, cos_ref, sin_ref, si_ref, *rest, sb, aliased):
    if aliased:
        rest = rest[1:]
    u_ref, so_ref, obuf = rest
    h = pl.program_id(1)
    cos = cos_ref[...]
    sin = sin_ref[...]
    gamma = jnp.exp(_log_decay(h, (1, 1)))
    q = _rope(q_ref[...], cos, sin)
    k = _rope(k_ref[...], cos, sin, RET_DK ** -0.5)
    v = v_ref[...]
    qk = jnp.sum(q * k, axis=1, keepdims=True)
    pad = jnp.zeros((LANES - sb, RET_DK), F32)
    q_t = jnp.concatenate([q * gamma, pad], axis=0).T
    k_t = jnp.concatenate([k, pad], axis=0).T
    for s in range(sb):
        state = si_ref[0, s, 0]
        vs = v[s:s + 1, :]
        o_state = jnp.sum(q_t[:, s:s + 1] * state, axis=0, keepdims=True)
        so_ref[0, s, 0] = gamma * state + k_t[:, s:s + 1] * vs
        obuf[s:s + 1, :] = qk[s:s + 1, :] * vs + o_state
    u_ref[...] = _group_norm_gate(obuf[...], g_ref[...])


def _ret_sample(proj, cos_s, sin_s, state_ret, j, prev_out):
    sb = 8
    r0 = TP // sb
    aliased = prev_out is not None
    in_specs = [pl.BlockSpec((sb, RET_DK), lambda i, h: (r0 + i, h)),
                pl.BlockSpec((sb, RET_DK), lambda i, h: (r0 + i, RET_HEADS + h)),
                pl.BlockSpec((sb, RET_DV), lambda i, h: (r0 + i, RET_HEADS + h)),
                pl.BlockSpec((sb, RET_DV), lambda i, h: (r0 + i, 2 * RET_HEADS + h)),
                pl.BlockSpec((1, RET_DK // 2), lambda i, h: (0, 0)),
                pl.BlockSpec((1, RET_DK // 2), lambda i, h: (0, 0)),
                pl.BlockSpec((1, sb, 1, RET_DK, RET_DV), lambda i, h: (j, i, h, 0, 0))]
    args = [proj, proj, proj, proj, cos_s, sin_s, state_ret]
    aliases = {}
    if aliased:
        in_specs.append(pl.BlockSpec(memory_space=pl.ANY))
        args.append(prev_out)
        aliases = {len(args) - 1: 1}
    return pl.pallas_call(
        functools.partial(_ret_sample_body, sb=sb, aliased=aliased),
        out_shape=(jax.ShapeDtypeStruct((DEC_BATCH, RET_HV), F32),
                   jax.ShapeDtypeStruct(state_ret.shape, F32)),
        grid=(DEC_BATCH // sb, RET_HEADS),
        in_specs=in_specs,
        out_specs=(pl.BlockSpec((sb, RET_DV), lambda i, h: (i, h)),
                   pl.BlockSpec((1, sb, 1, RET_DK, RET_DV), lambda i, h: (j, i, h, 0, 0))),
        scratch_shapes=[pltpu.VMEM((sb, RET_DV), F32)],
        input_output_aliases=aliases,
        compiler_params=_cparams(("arbitrary", "arbitrary")),
        name="ret_sample",
    )(*args)


def _out_proj_body(u_ref, w_ref, x_ref, gt_ref, o_ref, *, per_row, tm):
    y = _bdot(u_ref[...].astype(BF16), w_ref[0].astype(BF16))
    if per_row:
        gate = gt_ref[0]
    else:
        gate = gt_ref[0, pl.ds(pl.program_id(0) * tm // SEQ, 1), :]
    o_ref[...] = x_ref[...] + gate * y


def _out_proj(u, w_out, j, x_all, row_tile0, mod, layer, per_row):
    m = u.shape[0]
    tm = min(m, 1024)
    tn = 512
    gate_blk = 2 * (D_MODEL // tn)
    mrows = mod.shape[1]
    return pl.pallas_call(
        functools.partial(_out_proj_body, per_row=per_row, tm=tm),
        out_shape=jax.ShapeDtypeStruct((m, D_MODEL), F32),
        grid=(m // tm, D_MODEL // tn),
        in_specs=[pl.BlockSpec((tm, RET_HV), lambda i, n: (i, 0)),
                  pl.BlockSpec((1, RET_HV, tn), lambda i, n: (j, 0, n)),
                  pl.BlockSpec((tm, tn), lambda i, n: (row_tile0 + i, n)),
                  pl.BlockSpec((1, mrows, tn), lambda i, n: (layer, 0, gate_blk + n))],
        out_specs=pl.BlockSpec((tm, tn), lambda i, n: (i, n)),
        compiler_params=_cparams(("arbitrary", "arbitrary")),
        name="ret_out_proj",
    )(u, w_out, x_all, mod)


def _ffn_pre_body(xp_ref, xs_ref, shp, scp, shs, scs, g_ref, wr_ref, br_ref,
                  hw_ref, idx_ref, gate_ref, rank_ref, cnt_ref, carry):
    i = pl.program_id(0)
    tm = TOK_TILE

    @pl.when(i == 0)
    def _():
        carry[...] = jnp.zeros_like(carry)

    x = jnp.where(i == N_PROMPT_TILES, xs_ref[...], xp_ref[...])
    h = _norm_mod(x, g_ref[...], _pick_mod(i, shp, shs), _pick_mod(i, scp, scs))

    bits = lax.bitcast_convert_type(h.astype(BF16).astype(F32), U32)
    half = D_MODEL // 2
    hw_ref[...] = (bits[:, :half] >> 16) | (bits[:, half:] & jnp.uint32(0xFFFF0000))

    lane = lax.broadcasted_iota(I32, (tm, LANES), 1)
    lane_f = lane.astype(F32)
    logits = jnp.dot(h, wr_ref[0], preferred_element_type=F32, precision=lax.Precision.HIGHEST) + br_ref[0]
    logits = jnp.where(lane < N_EXPERTS, logits, -jnp.inf)

    hots, vals = [], []
    idx_out = jnp.zeros((tm, LANES), F32)
    for kk in range(TOP_K):
        m = jnp.max(logits, axis=1, keepdims=True)
        pick = jnp.min(jnp.where(logits == m, lane_f, float(LANES)), axis=1, keepdims=True)
        hot = lane_f == pick
        logits = jnp.where(hot, -jnp.inf, logits)
        hots.append(hot)
        vals.append(m)
        idx_out = jnp.where(lane == kk, pick, idx_out)
    exps = [jnp.exp(v - vals[0]) for v in vals]
    den = exps[0] + exps[1] + exps[2] + exps[3]
    gate_out = jnp.zeros((tm, LANES), F32)
    for kk in range(TOP_K):
        gate_out = jnp.where(lane == kk, exps[kk] / den, gate_out)

    cnt = jnp.zeros((tm, LANES), F32)
    for hot in hots:
        cnt = cnt + hot.astype(F32)
    r_i = lax.broadcasted_iota(I32, (tm, tm), 0)
    c_i = lax.broadcasted_iota(I32, (tm, tm), 1)
    lower = jnp.where(c_i < r_i, 1.0, 0.0).astype(BF16)
    base = _bdot(lower, cnt.astype(BF16)) + carry[0:1, :]
    rank_out = jnp.zeros((tm, LANES), F32)
    for kk in range(TOP_K):
        rk = jnp.sum(jnp.where(hots[kk], base, 0.0), axis=1, keepdims=True)
        rank_out = jnp.where(lane == kk, rk, rank_out)
    total = carry[0:1, :] + jnp.sum(cnt, axis=0, keepdims=True)
    carry[0:1, :] = total

    idx_ref[...] = idx_out.astype(I32)
    gate_ref[...] = gate_out
    rank_ref[...] = rank_out.astype(I32)

    @pl.when(i == pl.num_programs(0) - 1)
    def _():
        cnt_ref[...] = jnp.broadcast_to(total, cnt_ref.shape).astype(I32)


def _ffn_pre(x_p, x_s, mod_p, mod_s, layer, g, w_router_pad, b_router_pad):
    tm = TOK_TILE
    tok = lambda i: (i, 0)
    return pl.pallas_call(
        _ffn_pre_body,
        out_shape=(jax.ShapeDtypeStruct((T_ALL, D_MODEL // 2), U32),
                   jax.ShapeDtypeStruct((T_ALL, LANES), I32),
                   jax.ShapeDtypeStruct((T_ALL, LANES), F32),
                   jax.ShapeDtypeStruct((T_ALL, LANES), I32),
                   jax.ShapeDtypeStruct((8, LANES), I32)),
        grid=(N_TOK_TILES,),
        in_specs=[pl.BlockSpec((tm, D_MODEL), lambda i: (jnp.minimum(i, N_PROMPT_TILES - 1), 0)),
                  pl.BlockSpec((tm, D_MODEL), lambda i: (0, 0)),
                  pl.BlockSpec((1, 8, D_MODEL), lambda i: (layer, 0, 3)),
                  pl.BlockSpec((1, 8, D_MODEL), lambda i: (layer, 0, 4)),
                  pl.BlockSpec((1, DEC_BATCH, D_MODEL), lambda i: (layer, 0, 3)),
                  pl.BlockSpec((1, DEC_BATCH, D_MODEL), lambda i: (layer, 0, 4)),
                  pl.BlockSpec((1, D_MODEL), lambda i: (0, 0)),
                  pl.BlockSpec((1, D_MODEL, LANES), lambda i: (layer, 0, 0)),
                  pl.BlockSpec((1, 1, LANES), lambda i: (layer, 0, 0))],
        out_specs=(pl.BlockSpec((tm, D_MODEL // 2), tok),
                   pl.BlockSpec((tm, LANES), tok),
                   pl.BlockSpec((tm, LANES), tok),
                   pl.BlockSpec((tm, LANES), tok),
                   pl.BlockSpec((8, LANES), lambda i: (0, 0))),
        scratch_shapes=[pltpu.VMEM((8, LANES), F32)],
        compiler_params=_cparams(("arbitrary",)),
        name="ffn_pre",
    )(x_p, x_s, mod_p, mod_p, mod_s, mod_s, g, w_router_pad, b_router_pad)


def _route(idx, rank, counts):
    e = idx[:, :TOP_K]
    r = rank[:, :TOP_K]
    cnt = counts[0, :N_EXPERTS]
    ntile = (cnt + MOE_SUB - 1) // MOE_SUB
    tile_end = jnp.cumsum(ntile)
    tile_start = tile_end - ntile
    pos = (tile_start[e] * MOE_SUB + r).astype(I32)
    tok = jnp.broadcast_to(jnp.arange(T_ALL, dtype=I32)[:, None], pos.shape)
    row_tok = jnp.zeros((MOE_ROWS,), I32).at[pos.reshape(-1)].set(tok.reshape(-1))
    npass = (ntile + MOE_NSUB - 1) // MOE_NSUB
    item_end = jnp.cumsum(npass)
    item_start = item_end - npass
    n_items = item_end[-1]
    it = jnp.arange(MOE_ITEMS + 1, dtype=I32)
    it_c = jnp.minimum(it, n_items - 1)
    item_e = jnp.minimum(jnp.searchsorted(item_end, it_c, side="right"), N_EXPERTS - 1).astype(I32)
    p = it_c - item_start[item_e]
    item_row0 = ((tile_start[item_e] + p * MOE_NSUB) * MOE_SUB).astype(I32)
    item_nsub = jnp.where(it < n_items, jnp.minimum(ntile[item_e] - p * MOE_NSUB, MOE_NSUB), 0).astype(I32)
    return pos.reshape(-1), row_tok, item_e, item_row0, item_nsub, n_items.reshape(1).astype(I32)


def _experts_body(item_e, item_row0, item_nsub, n_items, row_tok,
                  hw_hbm, wg_ref, wu_ref, bg_ref, bu_ref, wd_ref, bd_ref,
                  ys_hbm,
                  xraw, hbuf, wgb, wub, wdb, ybuf, gsem, ysem):
    it = pl.program_id(0)
    j = pl.program_id(1)
    nsub = item_nsub[it]
    row0 = item_row0[it]
    nsub_next = item_nsub[it + 1]
    row0_next = item_row0[it + 1]
    n_pref = jnp.where(it > 0, jnp.minimum(item_nsub[jnp.maximum(it - 1, 0)], nsub), 0)
    active = it < n_items[0]
    ts = MOE_SUB
    tn = MOE_TN
    pf_rows = ts // MOE_JD

    def chunk(c):
        return slice(c * tn, (c + 1) * tn)

    def row_copy(base, r, s):
        return pltpu.make_async_copy(hw_hbm.at[row_tok[base + r]], xraw.at[r], gsem.at[s])

    def out_copy(s, slot, jd):
        dst = ys_hbm.at[pl.ds(pl.multiple_of(row0 + s * ts, ts), ts), pl.ds(pl.multiple_of(jd * tn, tn), tn)]
        return pltpu.make_async_copy(ybuf.at[slot], dst, ysem.at[slot])

    def swiglu(gt, up):
        gt = jnp.minimum(gt, SWIGLU_LIMIT)
        up = jnp.clip(up, -SWIGLU_LIMIT, SWIGLU_LIMIT)
        return ((up + 1.0) * gt * jax.nn.sigmoid(SWIGLU_ALPHA * gt)).astype(BF16)

    @pl.when(jnp.logical_and(active, j == 0))
    def _gather():
        def issue(r, c):
            row_copy(row0, r, lax.shift_right_logical(r, int(np.log2(ts)))).start()
            return c
        lax.fori_loop(n_pref * ts, nsub * ts, issue, 0)

        def land(s, c):
            rows = pl.ds(pl.multiple_of(s * ts, ts), ts)
            pltpu.make_async_copy(hw_hbm.at[pl.ds(0, ts)], xraw.at[rows], gsem.at[s]).wait()
            return c
        lax.fori_loop(0, nsub, land, 0)

    def unpack(w, high):
        bits = (w & jnp.uint32(0xFFFF0000)) if high else (w << 16)
        return lax.bitcast_convert_type(bits, F32).astype(BF16)

    @pl.when(jnp.logical_and(active, j < MOE_JGU))
    def _gate_up():
        per_half = (D_MODEL // 2) // tn
        gt = jnp.broadcast_to(bg_ref[0, 0], (ts, tn))
        up = jnp.broadcast_to(bu_ref[0, 0], (ts, tn))
        for c in range(D_MODEL // tn):
            wg_c = wg_ref[0, 0, chunk(c), :].astype(BF16)
            wu_c = wu_ref[0, 0, chunk(c), :].astype(BF16)
            wgb[chunk(c), :] = wg_c
            wub[chunk(c), :] = wu_c
            x_c = unpack(xraw[0:ts, chunk(c % per_half)], c >= per_half)
            gt = gt + _bdot(x_c, wg_c)
            up = up + _bdot(x_c, wu_c)
        hbuf[j, 0:ts, :] = swiglu(gt, up)

        def sub(s, c):
            rows = pl.ds(pl.multiple_of(s * ts, ts), ts)
            w = xraw[rows, :]
            x = jnp.concatenate([unpack(w, False), unpack(w, True)], axis=1)
            hbuf[j, rows, :] = swiglu(_bdot(x, wgb[...]) + bg_ref[0, 0], _bdot(x, wub[...]) + bu_ref[0, 0])
            return c
        lax.fori_loop(1, nsub, sub, 0)

    @pl.when(jnp.logical_and(active, j >= MOE_JGU))
    def _down():
        jd = j - MOE_JGU
        n_both = jnp.minimum(nsub, nsub_next)

        def down_tile(s, first, prefetch):
            rows = slice(0, ts) if first else pl.ds(pl.multiple_of(s * ts, ts), ts)
            slot = 0 if first else s % 2
            if not first:
                @pl.when(s >= 2)
                def _():
                    out_copy(s - 2, slot, jd).wait()

            acc = jnp.broadcast_to(bd_ref[0, 0], (ts, tn))
            for c in range(MOE_JGU):
                if first:
                    wd_c = wd_ref[0, 0, chunk(c), :].astype(BF16)
                    wdb[chunk(c), :] = wd_c
                else:
                    wd_c = wdb[chunk(c), :]
                acc = acc + _bdot(hbuf[c, rows, :], wd_c)
            ybuf[slot] = acc
            out_copy(s, slot, jd).start()
            if prefetch:
                for q in range(pf_rows):
                    row_copy(row0_next, s * ts + jd * pf_rows + q, s).start()

        @pl.when(n_both >= 1)
        def _():
            down_tile(0, True, True)

        @pl.when(n_both < 1)
        def _():
            down_tile(0, True, False)

        def sub_pf(s, c):
            down_tile(s, False, True)
            return c
        lax.fori_loop(1, n_both, sub_pf, 0)

        def sub(s, c):
            down_tile(s, False, False)
            return c
        lax.fori_loop(jnp.maximum(n_both, 1), nsub, sub, 0)

        @pl.when(nsub >= 2)
        def _():
            out_copy(nsub - 2, nsub % 2, jd).wait()

        @pl.when(nsub >= 1)
        def _():
            out_copy(nsub - 1, (nsub - 1) % 2, jd).wait()

    @pl.when(jnp.logical_and(it == pl.num_programs(0) - 1, j == pl.num_programs(1) - 1))
    def _fill_tail():
        ybuf[0] = jnp.zeros((ts, tn), F32)
        last = n_items[0] - 1
        used = item_row0[last] // ts + item_nsub[last]

        def tail_copy(t, c):
            dst = ys_hbm.at[pl.ds(pl.multiple_of(t * ts, ts), ts), pl.ds(c * tn, tn)]
            return pltpu.make_async_copy(ybuf.at[0], dst, ysem.at[0])

        def fill(t, carry):
            for c in range(MOE_JD):
                tail_copy(t, c).start()
            for c in range(MOE_JD):
                tail_copy(t, c).wait()
            return carry
        lax.fori_loop(used, MOE_ROW_TILES, fill, 0)


def _experts(route, hw, w_gate_up, b_gate_up, w_down, b_down, layer):
    _, row_tok, item_e, item_row0, item_nsub, n_items = route
    tn = MOE_TN
    jgu = MOE_JGU
    gu_c = lambda j: jnp.minimum(j, jgu - 1)
    dn_c = lambda j: jnp.maximum(j - jgu, 0)
    grid_spec = pltpu.PrefetchScalarGridSpec(
        num_scalar_prefetch=5,
        grid=(MOE_ITEMS, MOE_JGU + MOE_JD),
        in_specs=[
            pl.BlockSpec(memory_space=pl.ANY),
            pl.BlockSpec((1, 1, D_MODEL, tn), lambda it, j, ie, *_: (layer, ie[it], 0, gu_c(j))),
            pl.BlockSpec((1, 1, D_MODEL, tn), lambda it, j, ie, *_: (layer, ie[it], 0, jgu + gu_c(j))),
            pl.BlockSpec((1, 1, 1, tn), lambda it, j, ie, *_: (layer, ie[it], 0, gu_c(j))),
            pl.BlockSpec((1, 1, 1, tn), lambda it, j, ie, *_: (layer, ie[it], 0, jgu + gu_c(j))),
            pl.BlockSpec((1, 1, D_FF, tn), lambda it, j, ie, *_: (layer, ie[it], 0, dn_c(j))),
            pl.BlockSpec((1, 1, 1, tn), lambda it, j, ie, *_: (layer, ie[it], 0, dn_c(j))),
        ],
        out_specs=pl.BlockSpec(memory_space=pl.ANY),
        scratch_shapes=[
            pltpu.VMEM((MOE_NSUB * MOE_SUB, D_MODEL // 2), U32),
            pltpu.VMEM((MOE_JGU, MOE_NSUB * MOE_SUB, tn), BF16),
            pltpu.VMEM((D_MODEL, tn), BF16),
            pltpu.VMEM((D_MODEL, tn), BF16),
            pltpu.VMEM((D_FF, tn), BF16),
            pltpu.VMEM((2, MOE_SUB, tn), F32),
            pltpu.SemaphoreType.DMA((MOE_NSUB,)),
            pltpu.SemaphoreType.DMA((2,)),
        ],
    )
    ne = w_gate_up.shape[1]
    return pl.pallas_call(
        _experts_body,
        out_shape=jax.ShapeDtypeStruct((MOE_ROWS, D_MODEL), F32),
        grid_spec=grid_spec,
        compiler_params=_cparams(("arbitrary", "arbitrary")),
        name="moe_experts",
    )(item_e, item_row0, item_nsub, n_items, row_tok,
      hw, w_gate_up, w_gate_up,
      b_gate_up.reshape(DEPTH, ne, 1, 2 * D_FF), b_gate_up.reshape(DEPTH, ne, 1, 2 * D_FF),
      w_down, b_down.reshape(DEPTH, ne, 1, D_MODEL))


def _combine_body(pos, ys_hbm, xp_ref, xs_ref, gate_ref, gtp, gts, o_ref, ybuf, sem):
    i = pl.program_id(0)
    tm = TOK_TILE
    base = i * (tm * TOP_K)

    def issue(t, c):
        for kk in range(TOP_K):
            pltpu.make_async_copy(ys_hbm.at[pos[base + t * TOP_K + kk]], ybuf.at[kk, t], sem).start()
        return c
    lax.fori_loop(0, tm, issue, 0)
    for kk in range(TOP_K):
        pltpu.make_async_copy(ys_hbm.at[pl.ds(0, tm)], ybuf.at[kk], sem).wait()

    gates = gate_ref[...]
    f = gates[:, 0:1] * ybuf[0]
    for kk in range(1, TOP_K):
        f = f + gates[:, kk:kk + 1] * ybuf[kk]
    x = jnp.where(i == N_PROMPT_TILES, xs_ref[...], xp_ref[...])
    o_ref[...] = x + _pick_mod(i, gtp, gts) * f


def _combine(pos, ys, x_p, x_s, gates, mod_p, mod_s, layer):
    tm = TOK_TILE
    grid_spec = pltpu.PrefetchScalarGridSpec(
        num_scalar_prefetch=1,
        grid=(N_TOK_TILES,),
        in_specs=[pl.BlockSpec(memory_space=pl.ANY),
                  pl.BlockSpec((tm, D_MODEL), lambda i, *_: (jnp.minimum(i, N_PROMPT_TILES - 1), 0)),
                  pl.BlockSpec((tm, D_MODEL), lambda i, *_: (0, 0)),
                  pl.BlockSpec((tm, LANES), lambda i, *_: (i, 0)),
                  pl.BlockSpec((1, 8, D_MODEL), lambda i, *_: (layer, 0, 5)),
                  pl.BlockSpec((1, DEC_BATCH, D_MODEL), lambda i, *_: (layer, 0, 5))],
        out_specs=pl.BlockSpec((tm, D_MODEL), lambda i, *_: (i, 0)),
        scratch_shapes=[pltpu.VMEM((TOP_K, tm, D_MODEL), F32), pltpu.SemaphoreType.DMA],
    )
    return pl.pallas_call(
        _combine_body,
        out_shape=jax.ShapeDtypeStruct((T_ALL, D_MODEL), F32),
        grid_spec=grid_spec,
        compiler_params=_cparams(("arbitrary",)),
        name="moe_combine",
    )(pos, ys, x_p, x_s, gates, mod_p, mod_s)


def _final_norm_body(x_ref, g_ref, op_ref, os_ref):
    i = pl.program_id(0)
    x = x_ref[...]
    y = x * lax.rsqrt(jnp.mean(x * x, axis=-1, keepdims=True) + NORM_EPS) * g_ref[...]

    @pl.when(i < N_PROMPT_TILES)
    def _():
        op_ref[...] = y

    @pl.when(i == N_PROMPT_TILES)
    def _():
        os_ref[...] = y


def _final_norm(x_all, g):
    tm = TOK_TILE
    return pl.pallas_call(
        _final_norm_body,
        out_shape=(jax.ShapeDtypeStruct((TP, D_MODEL), F32),
                   jax.ShapeDtypeStruct((DEC_BATCH, D_MODEL), F32)),
        grid=(N_TOK_TILES,),
        in_specs=[pl.BlockSpec((tm, D_MODEL), lambda i: (i, 0)),
                  pl.BlockSpec((1, D_MODEL), lambda i: (0, 0))],
        out_specs=(pl.BlockSpec((tm, D_MODEL), lambda i: (jnp.minimum(i, N_PROMPT_TILES - 1), 0)),
                   pl.BlockSpec((tm, D_MODEL), lambda i: (0, 0))),
        compiler_params=_cparams(("arbitrary",)),
        name="final_norm",
    )(x_all, g)


def _rope_tables(pos):
    inv = ROPE_BASE ** (-jnp.linspace(0.0, 1.0, RET_DK // 2, dtype=F32))
    ang = pos.astype(F32)[:, None] * inv[None, :]
    return jnp.cos(ang), jnp.sin(ang)


def kernel(x_prompt, x_sample, state_pool, state_ret, c_prompt, c_sample, w_mod, b_mod, g_norm_mix, g_norm_ffn,
           pool_w, pool_scale, ret_w_in, ret_w_out, moe_w_router, moe_b_router, moe_w_gate_up, moe_b_gate_up,
           moe_w_down, moe_b_down, g_final):
    assert x_prompt.shape == (BATCH, SEQ, D_MODEL) and x_sample.shape == (DEC_BATCH, 1, D_MODEL)
    c_prompt8 = jnp.concatenate([c_prompt, jnp.zeros((8 - BATCH, D_MODEL), F32)], axis=0)
    mod_s, mod_p = _adaln(c_sample, c_prompt8, w_mod, b_mod)
    cos_p, sin_p = _rope_tables(jnp.arange(SEQ))
    cos_s, sin_s = _rope_tables(PAST_LEN + jnp.arange(1))
    w_router_pad = jnp.pad(moe_w_router, ((0, 0), (0, 0), (0, LANES - N_EXPERTS)))
    b_router_pad = jnp.pad(moe_b_router, ((0, 0), (0, LANES - N_EXPERTS))).reshape(DEPTH, 1, LANES)

    x_p_src, x_p_tile0 = x_prompt.reshape(TP, D_MODEL), 0
    x_s_src, x_s_rows0 = x_sample.reshape(DEC_BATCH, D_MODEL), 0
    x_all = None
    new_pool_p, new_pool_s, new_ret_p = [], [], []
    ret_s_out = None
    for layer in range(DEPTH):
        j = layer // N_MIXERS
        g_mix = g_norm_mix[layer].reshape(1, D_MODEL)
        g_ffn = g_norm_ffn[layer].reshape(1, D_MODEL)
        if layer % N_MIXERS == 0:
            scale = pool_scale[j].reshape(1, D_MODEL)
            x_p, pool_p = _pool_prompt(x_p_src, x_p_tile0, mod_p, layer, g_mix, pool_w, scale, j)
            x_s, pool_s = _pool_sample(x_s_src, x_s_rows0 // 16, state_pool, mod_s, layer, g_mix, pool_w, scale, j)
            new_pool_p.append(pool_p)
            new_pool_s.append(pool_s)
        else:
            h_all = _norm_mod_all(x_all, mod_p, mod_s, layer, g_mix)
            proj = _in_proj(h_all, ret_w_in, j)
            u_p, ret_p = _ret_prompt(proj, cos_p, sin_p)
            u_s, ret_s_out = _ret_sample(proj, cos_s, sin_s, state_ret, j, ret_s_out)
            new_ret_p.append(ret_p)
            x_p = _out_proj(u_p, ret_w_out, j, x_all, 0, mod_p, layer, per_row=False)
            x_s = _out_proj(u_s, ret_w_out, j, x_all, TP // DEC_BATCH, mod_s, layer, per_row=True)
        hw, idx, gates, rank, counts = _ffn_pre(x_p, x_s, mod_p, mod_s, layer, g_ffn, w_router_pad, b_router_pad)
        route = _route(idx, rank, counts)
        ys = _experts(route, hw, moe_w_gate_up, moe_b_gate_up, moe_w_down, moe_b_down, layer)
        x_all = _combine(route[0], ys, x_p, x_s, gates, mod_p, mod_s, layer)
        x_p_src, x_p_tile0 = x_all, 0
        x_s_src, x_s_rows0 = x_all, TP
    y_p, y_s = _final_norm(x_all, g_final.reshape(1, D_MODEL))
    return (y_p.reshape(BATCH, SEQ, D_MODEL), y_s.reshape(DEC_BATCH, 1, D_MODEL),
            jnp.stack(new_pool_p), jnp.stack(new_pool_s), jnp.stack(new_ret_p), ret_s_out)
```

```python
import functools

import jax
import jax.numpy as jnp
import numpy as np
from jax import lax
from jax.experimental import pallas as pl
from jax.experimental.pallas import tpu as pltpu

F32 = jnp.float32
BF16 = jnp.bfloat16
I32 = jnp.int32
U32 = jnp.uint32

D_MODEL = 2048
BATCH = 4
SEQ = 2048
DEPTH = 4
DEC_BATCH = 128
PAST_LEN = 16384
N_MIXERS = 2
POOL_WINDOWS = (2, 4, 8, 16)
POOL_GROUP_DIM = D_MODEL // len(POOL_WINDOWS)
POOL_BUF = max(POOL_WINDOWS) - 1
POOL_HALO = 16
RET_HEADS = 8
RET_DK = D_MODEL // RET_HEADS
RET_DV = 2 * D_MODEL // RET_HEADS
RET_CHUNK = 128
RET_HK = RET_HEADS * RET_DK
RET_HV = RET_HEADS * RET_DV
RET_IN_DIM = 2 * RET_HK + 2 * RET_HV
ROPE_BASE = 10000.0
N_EXPERTS = 32
TOP_K = 4
D_FF = D_MODEL
SWIGLU_LIMIT = 7.0
SWIGLU_ALPHA = 1.702
NORM_EPS = 1e-5

TP = BATCH * SEQ
T_ALL = TP + DEC_BATCH
TOK_TILE = 128
N_TOK_TILES = T_ALL // TOK_TILE
N_PROMPT_TILES = TP // TOK_TILE
TILES_PER_SEQ = SEQ // TOK_TILE
LANES = 128

MOE_SUB = 256
MOE_NSUB = 8
MOE_TN = 512
MOE_JGU = D_FF // MOE_TN
MOE_JD = D_MODEL // MOE_TN
N_ASSIGN = T_ALL * TOP_K
MOE_ROW_TILES = -(-(N_ASSIGN + N_EXPERTS * (MOE_SUB - 1)) // MOE_SUB)
MOE_ROWS = MOE_ROW_TILES * MOE_SUB
MOE_ITEMS = N_EXPERTS + MOE_ROW_TILES // MOE_NSUB

VMEM_LIMIT = 56 * 1024 * 1024


def _cparams(sem, vmem=VMEM_LIMIT):
    return pltpu.CompilerParams(dimension_semantics=sem, vmem_limit_bytes=vmem)


def _norm_mod(x, g, shift, scale):
    ms = jnp.mean(x * x, axis=-1, keepdims=True)
    y = x * lax.rsqrt(ms + NORM_EPS)
    return (y * g) * (1.0 + scale) + shift


def _silu(x):
    return x * jax.nn.sigmoid(x)


def _bdot(a, b):
    return jnp.dot(a, b, preferred_element_type=F32)


def _adaln_body(cs_ref, cp_ref, w_ref, b_ref, os_ref, op_ref):
    w = w_ref[0].astype(BF16)
    b = b_ref[0]
    os_ref[0] = _bdot(_silu(cs_ref[...]).astype(BF16), w) + b
    op_ref[0] = _bdot(_silu(cp_ref[...]).astype(BF16), w) + b


def _adaln(c_sample, c_prompt8, w_mod, b_mod):
    nl, d, n6 = w_mod.shape
    tn = 1024
    return pl.pallas_call(
        _adaln_body,
        out_shape=(jax.ShapeDtypeStruct((nl, DEC_BATCH, n6), F32),
                   jax.ShapeDtypeStruct((nl, 8, n6), F32)),
        grid=(nl, n6 // tn),
        in_specs=[pl.BlockSpec((DEC_BATCH, d), lambda l, n: (0, 0)),
                  pl.BlockSpec((8, d), lambda l, n: (0, 0)),
                  pl.BlockSpec((1, d, tn), lambda l, n: (l, 0, n)),
                  pl.BlockSpec((1, 1, tn), lambda l, n: (l, 0, n))],
        out_specs=(pl.BlockSpec((1, DEC_BATCH, tn), lambda l, n: (l, 0, n)),
                   pl.BlockSpec((1, 8, tn), lambda l, n: (l, 0, n))),
        compiler_params=_cparams(("arbitrary", "arbitrary")),
        name="adaln",
    )(c_sample, c_prompt8, w_mod, b_mod.reshape(nl, 1, n6))


def _pick_mod(i, mp_ref, ms_ref):
    b = jnp.minimum(i // TILES_PER_SEQ, BATCH - 1)
    return jnp.where(i == N_PROMPT_TILES, ms_ref[0], mp_ref[0, pl.ds(b, 1), :])


def _pool_group_dots(diff, pw_ref):
    ys = []
    for gi in range(len(POOL_WINDOWS)):
        cols = slice(gi * POOL_GROUP_DIM, (gi + 1) * POOL_GROUP_DIM)
        ys.append(_bdot(diff[gi] if isinstance(diff, (list, tuple)) else diff[:, cols],
                        pw_ref[0, gi].astype(BF16)))
    return jnp.concatenate(ys, axis=1)


def _pool_prompt_body(x_ref, sh_ref, sc_ref, gt_ref, g_ref, pw_ref, ps_ref, xo_ref, np_ref, hbuf, *, ts):
    b = pl.program_id(0)
    s = pl.program_id(1)

    @pl.when(s == 0)
    def _():
        hbuf[0:POOL_HALO, :] = jnp.zeros((POOL_HALO, D_MODEL), F32)

    x = x_ref[...]
    shift = sh_ref[0, pl.ds(b, 1), :]
    scale = sc_ref[0, pl.ds(b, 1), :]
    gate = gt_ref[0, pl.ds(b, 1), :]
    h = _norm_mod(x, g_ref[...], shift, scale)
    hbuf[POOL_HALO:, :] = h
    t = s * ts + lax.broadcasted_iota(I32, (ts, 1), 0)
    diffs = []
    for gi, w in enumerate(POOL_WINDOWS):
        cols = slice(gi * POOL_GROUP_DIM, (gi + 1) * POOL_GROUP_DIM)
        acc = hbuf[:, cols]
        k = 1
        while k < w:
            acc = acc + pltpu.roll(acc, k, 0)
            k *= 2
        cnt = jnp.minimum(t + 1, w).astype(F32)
        diffs.append((acc[POOL_HALO:, :] / cnt - h[:, cols]).astype(BF16))
    y = _pool_group_dots(diffs, pw_ref) * ps_ref[...]
    xo_ref[...] = x + gate * y

    @pl.when(s == pl.num_programs(1) - 1)
    def _():
        np_ref[0] = hbuf[pl.ds(POOL_HALO + ts - POOL_BUF, POOL_BUF), :]

    hbuf[0:POOL_HALO, :] = hbuf[ts:ts + POOL_HALO, :]


def _pool_prompt(x2d, row_tile0, mod_p, layer, g, pool_w, pool_scale, j):
    ts = 256
    ns = SEQ // ts
    mspec = lambda k: pl.BlockSpec((1, 8, D_MODEL), lambda b, s: (layer, 0, k))
    return pl.pallas_call(
        functools.partial(_pool_prompt_body, ts=ts),
        out_shape=(jax.ShapeDtypeStruct((TP, D_MODEL), F32),
                   jax.ShapeDtypeStruct((BATCH, POOL_BUF, D_MODEL), F32)),
        grid=(BATCH, ns),
        in_specs=[pl.BlockSpec((ts, D_MODEL), lambda b, s: (row_tile0 + b * ns + s, 0)),
                  mspec(0), mspec(1), mspec(2),
                  pl.BlockSpec((1, D_MODEL), lambda b, s: (0, 0)),
                  pl.BlockSpec((1, len(POOL_WINDOWS), POOL_GROUP_DIM, POOL_GROUP_DIM),
                               lambda b, s: (j, 0, 0, 0)),
                  pl.BlockSpec((1, D_MODEL), lambda b, s: (0, 0))],
        out_specs=(pl.BlockSpec((ts, D_MODEL), lambda b, s: (b * ns + s, 0)),
                   pl.BlockSpec((1, POOL_BUF, D_MODEL), lambda b, s: (b, 0, 0))),
        scratch_shapes=[pltpu.VMEM((POOL_HALO + ts, D_MODEL), F32)],
        compiler_params=_cparams(("arbitrary", "arbitrary")),
        name="pool_prompt",
    )(x2d, mod_p, mod_p, mod_p, g, pool_w, pool_scale)


def _pool_sample_body(x_ref, st_ref, sh_ref, sc_ref, gt_ref, g_ref, pw_ref, ps_ref, xo_ref, npo_ref, dbuf, *, sb):
    x = x_ref[...]
    h = _norm_mod(x, g_ref[...], sh_ref[0], sc_ref[0])
    row = lax.broadcasted_iota(I32, (POOL_BUF, 1), 0)
    for s in range(sb):
        st = st_ref[0, s]
        hs = h[s:s + 1, :]
        parts = []
        for gi, w in enumerate(POOL_WINDOWS):
            cols = slice(gi * POOL_GROUP_DIM, (gi + 1) * POOL_GROUP_DIM)
            tail = jnp.sum(jnp.where(row >= POOL_BUF - (w - 1), st[:, cols], 0.0), axis=0, keepdims=True)
            parts.append((tail + hs[:, cols]) / float(w) - hs[:, cols])
        dbuf[s:s + 1, :] = jnp.concatenate(parts, axis=1)
        npo_ref[s, 0:POOL_BUF - 1, :] = st[1:POOL_BUF, :]
        npo_ref[s, POOL_BUF - 1:POOL_BUF, :] = hs
    y = _pool_group_dots(dbuf[...].astype(BF16), pw_ref) * ps_ref[...]
    xo_ref[...] = x + gt_ref[0] * y


def _pool_sample(x2d, row_tile0, state_pool, mod_s, layer, g, pool_w, pool_scale, j):
    sb = 16
    mspec = lambda k: pl.BlockSpec((1, sb, D_MODEL), lambda i: (layer, i, k))
    return pl.pallas_call(
        functools.partial(_pool_sample_body, sb=sb),
        out_shape=(jax.ShapeDtypeStruct((DEC_BATCH, D_MODEL), F32),
                   jax.ShapeDtypeStruct((DEC_BATCH, POOL_BUF, D_MODEL), F32)),
        grid=(DEC_BATCH // sb,),
        in_specs=[pl.BlockSpec((sb, D_MODEL), lambda i: (row_tile0 + i, 0)),
                  pl.BlockSpec((1, sb, POOL_BUF, D_MODEL), lambda i: (j, i, 0, 0)),
                  mspec(0), mspec(1), mspec(2),
                  pl.BlockSpec((1, D_MODEL), lambda i: (0, 0)),
                  pl.BlockSpec((1, len(POOL_WINDOWS), POOL_GROUP_DIM, POOL_GROUP_DIM),
                               lambda i: (j, 0, 0, 0)),
                  pl.BlockSpec((1, D_MODEL), lambda i: (0, 0))],
        out_specs=(pl.BlockSpec((sb, D_MODEL), lambda i: (i, 0)),
                   pl.BlockSpec((sb, POOL_BUF, D_MODEL), lambda i: (i, 0, 0))),
        scratch_shapes=[pltpu.VMEM((sb, D_MODEL), F32)],
        compiler_params=_cparams(("arbitrary",)),
        name="pool_sample",
    )(x2d, state_pool, mod_s, mod_s, mod_s, g, pool_w, pool_scale)


def _norm_mod_body(x_ref, shp, scp, shs, scs, g_ref, o_ref):
    i = pl.program_id(0)
    h = _norm_mod(x_ref[...], g_ref[...], _pick_mod(i, shp, shs), _pick_mod(i, scp, scs))
    o_ref[...] = h.astype(BF16)


def _norm_mod_all(x_all, mod_p, mod_s, layer, g):
    return pl.pallas_call(
        _norm_mod_body,
        out_shape=jax.ShapeDtypeStruct((T_ALL, D_MODEL), BF16),
        grid=(N_TOK_TILES,),
        in_specs=[pl.BlockSpec((TOK_TILE, D_MODEL), lambda i: (i, 0)),
                  pl.BlockSpec((1, 8, D_MODEL), lambda i: (layer, 0, 0)),
                  pl.BlockSpec((1, 8, D_MODEL), lambda i: (layer, 0, 1)),
                  pl.BlockSpec((1, DEC_BATCH, D_MODEL), lambda i: (layer, 0, 0)),
                  pl.BlockSpec((1, DEC_BATCH, D_MODEL), lambda i: (layer, 0, 1)),
                  pl.BlockSpec((1, D_MODEL), lambda i: (0, 0))],
        out_specs=pl.BlockSpec((TOK_TILE, D_MODEL), lambda i: (i, 0)),
        compiler_params=_cparams(("arbitrary",)),
        name="ret_norm_mod",
    )(x_all, mod_p, mod_p, mod_s, mod_s, g)


def _in_proj_body(a_ref, w_ref, o_ref):
    o_ref[...] = _bdot(a_ref[...], w_ref[0].astype(BF16))


def _in_proj(h_all, w_in, j):
    tm = T_ALL // 4
    tn = 512
    return pl.pallas_call(
        _in_proj_body,
        out_shape=jax.ShapeDtypeStruct((T_ALL, RET_IN_DIM), F32),
        grid=(T_ALL // tm, RET_IN_DIM // tn),
        in_specs=[pl.BlockSpec((tm, D_MODEL), lambda m, n: (m, 0)),
                  pl.BlockSpec((1, D_MODEL, tn), lambda m, n: (j, 0, n))],
        out_specs=pl.BlockSpec((tm, tn), lambda m, n: (m, n)),
        compiler_params=_cparams(("arbitrary", "arbitrary")),
        name="ret_in_proj",
    )(h_all, w_in)


def _rope(x, cos, sin, scale=None):
    half = RET_DK // 2
    x1 = x[:, :half]
    x2 = x[:, half:]
    out = jnp.concatenate([x1 * cos - x2 * sin, x1 * sin + x2 * cos], axis=1)
    return out if scale is None else out * scale


def _log_decay(h, shape):
    hf = jnp.full(shape, h, I32).astype(F32)
    return jnp.log(1.0 - jnp.exp2(-5.0 - hf))


def _group_norm_gate(o, g):
    o = o * lax.rsqrt(jnp.mean(o * o, axis=-1, keepdims=True) + NORM_EPS)
    return _silu(g) * o


def _ret_prompt_body(q_ref, k_ref, v_ref, g_ref, cos_ref, sin_ref, u_ref, s_ref):
    c = pl.program_id(1)
    cn = RET_CHUNK

    @pl.when(c == 0)
    def _():
        s_ref[...] = jnp.zeros_like(s_ref)

    cos = cos_ref[...]
    sin = sin_ref[...]
    i_col = lax.broadcasted_iota(I32, (cn, 1), 0).astype(F32)
    j_row = lax.broadcasted_iota(I32, (1, cn), 1).astype(F32)
    rel = i_col - j_row
    for h in range(RET_HEADS):
        kcols = slice(h * RET_DK, (h + 1) * RET_DK)
        vcols = slice(h * RET_DV, (h + 1) * RET_DV)
        q = _rope(q_ref[:, kcols], cos, sin)
        k = _rope(k_ref[:, kcols], cos, sin, RET_DK ** -0.5)
        v = v_ref[:, vcols].astype(BF16)
        intra = jnp.where(rel >= 0, jnp.exp(_log_decay(h, (cn, cn)) * jnp.maximum(rel, 0.0)), 0.0)
        lg_col = _log_decay(h, (cn, 1))
        q_dec = jnp.exp(lg_col * (i_col + 1.0))
        k_dec = jnp.exp(lg_col * (cn - 1.0 - i_col))
        c_dec = jnp.exp(_log_decay(h, (1, 1)) * float(cn))

        state = s_ref[0, h]
        s = lax.dot_general(q.astype(BF16), k.astype(BF16), (((1,), (1,)), ((), ())),
                            preferred_element_type=F32) * intra
        o = _bdot(s.astype(BF16), v) + _bdot((q * q_dec).astype(BF16), state.astype(BF16))
        kd_t = (k * k_dec).T.astype(BF16)
        s_ref[0, h] = c_dec * state + _bdot(kd_t, v)
        u_ref[:, vcols] = _group_norm_gate(o, g_ref[:, vcols]).astype(BF16)


def _ret_prompt(proj, cos, sin):
    nc = SEQ // RET_CHUNK
    cn = RET_CHUNK
    row = lambda b, c: b * nc + c
    return pl.pallas_call(
        _ret_prompt_body,
        out_shape=(jax.ShapeDtypeStruct((TP, RET_HV), BF16),
                   jax.ShapeDtypeStruct((BATCH, RET_HEADS, RET_DK, RET_DV), F32)),
        grid=(BATCH, nc),
        in_specs=[pl.BlockSpec((cn, RET_HK), lambda b, c: (row(b, c), 0)),
                  pl.BlockSpec((cn, RET_HK), lambda b, c: (row(b, c), 1)),
                  pl.BlockSpec((cn, RET_HV), lambda b, c: (row(b, c), 1)),
                  pl.BlockSpec((cn, RET_HV), lambda b, c: (row(b, c), 2)),
                  pl.BlockSpec((cn, RET_DK // 2), lambda b, c: (c, 0)),
                  pl.BlockSpec((cn, RET_DK // 2), lambda b, c: (c, 0))],
        out_specs=(pl.BlockSpec((cn, RET_HV), lambda b, c: (row(b, c), 0)),
                   pl.BlockSpec((1, RET_HEADS, RET_DK, RET_DV), lambda b, c: (b, 0, 0, 0))),
        compiler_params=_cparams(("arbitrary", "arbitrary")),
        name="ret_prompt",
    )(proj, proj, proj, proj, cos, sin)


def _ret_sample_body(q_ref, k_ref, v_ref, g_ref, cos_ref, sin_ref, si_ref, *rest, sb, aliased):
    if aliased:
        rest = rest[1:]
    u_ref, so_ref, obuf = rest
    h = pl.program_id(1)
    cos = cos_ref[...]
    sin = sin_ref[...]
    gamma = jnp.exp(_log_decay(h, (1, 1)))
    q = _rope(q_ref[...], cos, sin)
    k = _rope(k_ref[...], cos, sin, RET_DK ** -0.5)
    v = v_ref[...]
    qk = jnp.sum(q * k, axis=1, keepdims=True)
    pad = jnp.zeros((LANES - sb, RET_DK), F32)
    q_t = jnp.concatenate([q * gamma, pad], axis=0).T
    k_t = jnp.concatenate([k, pad], axis=0).T
    for s in range(sb):
        state = si_ref[0, s, 0]
        vs = v[s:s + 1, :]
        o_state = jnp.sum(q_t[:, s:s + 1] * state, axis=0, keepdims=True)
        so_ref[0, s, 0] = gamma * state + k_t[:, s:s + 1] * vs
        obuf[s:s + 1, :] = qk[s:s + 1, :] * vs + o_state
    u_ref[...] = _group_norm_gate(obuf[...], g_ref[...])


def _ret_sample(proj, cos_s, sin_s, state_ret, j, prev_out):
    sb = 8
    r0 = TP // sb
    aliased = prev_out is not None
    in_specs = [pl.BlockSpec((sb, RET_DK), lambda i, h: (r0 + i, h)),
                pl.BlockSpec((sb, RET_DK), lambda i, h: (r0 + i, RET_HEADS + h)),
                pl.BlockSpec((sb, RET_DV), lambda i, h: (r0 + i, RET_HEADS + h)),
                pl.BlockSpec((sb, RET_DV), lambda i, h: (r0 + i, 2 * RET_HEADS + h)),
                pl.BlockSpec((1, RET_DK // 2), lambda i, h: (0, 0)),
                pl.BlockSpec((1, RET_DK // 2), lambda i, h: (0, 0)),
                pl.BlockSpec((1, sb, 1, RET_DK, RET_DV), lambda i, h: (j, i, h, 0, 0))]
    args = [proj, proj, proj, proj, cos_s, sin_s, state_ret]
    aliases = {}
    if aliased:
        in_specs.append(pl.BlockSpec(memory_space=pl.ANY))
        args.append(prev_out)
        aliases = {len(args) - 1: 1}
    return pl.pallas_call(
        functools.partial(_ret_sample_body, sb=sb, aliased=aliased),
        out_shape=(jax.ShapeDtypeStruct((DEC_BATCH, RET_HV), F32),
                   jax.ShapeDtypeStruct(state_ret.shape, F32)),
        grid=(DEC_BATCH // sb, RET_HEADS),
        in_specs=in_specs,
        out_specs=(pl.BlockSpec((sb, RET_DV), lambda i, h: (i, h)),
                   pl.BlockSpec((1, sb, 1, RET_DK, RET_DV), lambda i, h: (j, i, h, 0, 0))),
        scratch_shapes=[pltpu.VMEM((sb, RET_DV), F32)],
        input_output_aliases=aliases,
        compiler_params=_cparams(("arbitrary", "arbitrary")),
        name="ret_sample",
    )(*args)


def _out_proj_body(u_ref, w_ref, x_ref, gt_ref, o_ref, *, per_row, tm):
    y = _bdot(u_ref[...].astype(BF16), w_ref[0].astype(BF16))
    if per_row:
        gate = gt_ref[0]
    else:
        gate = gt_ref[0, pl.ds(pl.program_id(0) * tm // SEQ, 1), :]
    o_ref[...] = x_ref[...] + gate * y


def _out_proj(u, w_out, j, x_all, row_tile0, mod, layer, per_row):
    m = u.shape[0]
    tm = min(m, 1024)
    tn = 512
    gate_blk = 2 * (D_MODEL // tn)
    mrows = mod.shape[1]
    return pl.pallas_call(
        functools.partial(_out_proj_body, per_row=per_row, tm=tm),
        out_shape=jax.ShapeDtypeStruct((m, D_MODEL), F32),
        grid=(m // tm, D_MODEL // tn),
        in_specs=[pl.BlockSpec((tm, RET_HV), lambda i, n: (i, 0)),
                  pl.BlockSpec((1, RET_HV, tn), lambda i, n: (j, 0, n)),
                  pl.BlockSpec((tm, tn), lambda i, n: (row_tile0 + i, n)),
                  pl.BlockSpec((1, mrows, tn), lambda i, n: (layer, 0, gate_blk + n))],
        out_specs=pl.BlockSpec((tm, tn), lambda i, n: (i, n)),
        compiler_params=_cparams(("arbitrary", "arbitrary")),
        name="ret_out_proj",
    )(u, w_out, x_all, mod)


def _ffn_pre_body(xp_ref, xs_ref, shp, scp, shs, scs, g_ref, wr_ref, br_ref,
                  hw_ref, idx_ref, gate_ref, rank_ref, cnt_ref, carry):
    i = pl.program_id(0)
    tm = TOK_TILE

    @pl.when(i == 0)
    def _():
        carry[...] = jnp.zeros_like(carry)

    x = jnp.where(i == N_PROMPT_TILES, xs_ref[...], xp_ref[...])
    h = _norm_mod(x, g_ref[...], _pick_mod(i, shp, shs), _pick_mod(i, scp, scs))

    bits = lax.bitcast_convert_type(h.astype(BF16).astype(F32), U32)
    half = D_MODEL // 2
    hw_ref[...] = (bits[:, :half] >> 16) | (bits[:, half:] & jnp.uint32(0xFFFF0000))

    lane = lax.broadcasted_iota(I32, (tm, LANES), 1)
    lane_f = lane.astype(F32)
    logits = jnp.dot(h, wr_ref[0], preferred_element_type=F32, precision=lax.Precision.HIGHEST) + br_ref[0]
    logits = jnp.where(lane < N_EXPERTS, logits, -jnp.inf)

    hots, vals = [], []
    idx_out = jnp.zeros((tm, LANES), F32)
    for kk in range(TOP_K):
        m = jnp.max(logits, axis=1, keepdims=True)
        pick = jnp.min(jnp.where(logits == m, lane_f, float(LANES)), axis=1, keepdims=True)
        hot = lane_f == pick
        logits = jnp.where(hot, -jnp.inf, logits)
        hots.append(hot)
        vals.append(m)
        idx_out = jnp.where(lane == kk, pick, idx_out)
    exps = [jnp.exp(v - vals[0]) for v in vals]
    den = exps[0] + exps[1] + exps[2] + exps[3]
    gate_out = jnp.zeros((tm, LANES), F32)
    for kk in range(TOP_K):
        gate_out = jnp.where(lane == kk, exps[kk] / den, gate_out)

    cnt = jnp.zeros((tm, LANES), F32)
    for hot in hots:
        cnt = cnt + hot.astype(F32)
    r_i = lax.broadcasted_iota(I32, (tm, tm), 0)
    c_i = lax.broadcasted_iota(I32, (tm, tm), 1)
    lower = jnp.where(c_i < r_i, 1.0, 0.0).astype(BF16)
    base = _bdot(lower, cnt.astype(BF16)) + carry[0:1, :]
    rank_out = jnp.zeros((tm, LANES), F32)
    for kk in range(TOP_K):
        rk = jnp.sum(jnp.where(hots[kk], base, 0.0), axis=1, keepdims=True)
        rank_out = jnp.where(lane == kk, rk, rank_out)
    total = carry[0:1, :] + jnp.sum(cnt, axis=0, keepdims=True)
    carry[0:1, :] = total

    idx_ref[...] = idx_out.astype(I32)
    gate_ref[...] = gate_out
    rank_ref[...] = rank_out.astype(I32)

    @pl.when(i == pl.num_programs(0) - 1)
    def _():
        cnt_ref[...] = jnp.broadcast_to(total, cnt_ref.shape).astype(I32)


def _ffn_pre(x_p, x_s, mod_p, mod_s, layer, g, w_router_pad, b_router_pad):
    tm = TOK_TILE
    tok = lambda i: (i, 0)
    return pl.pallas_call(
        _ffn_pre_body,
        out_shape=(jax.ShapeDtypeStruct((T_ALL, D_MODEL // 2), U32),
                   jax.ShapeDtypeStruct((T_ALL, LANES), I32),
                   jax.ShapeDtypeStruct((T_ALL, LANES), F32),
                   jax.ShapeDtypeStruct((T_ALL, LANES), I32),
                   jax.ShapeDtypeStruct((8, LANES), I32)),
        grid=(N_TOK_TILES,),
        in_specs=[pl.BlockSpec((tm, D_MODEL), lambda i: (jnp.minimum(i, N_PROMPT_TILES - 1), 0)),
                  pl.BlockSpec((tm, D_MODEL), lambda i: (0, 0)),
                  pl.BlockSpec((1, 8, D_MODEL), lambda i: (layer, 0, 3)),
                  pl.BlockSpec((1, 8, D_MODEL), lambda i: (layer, 0, 4)),
                  pl.BlockSpec((1, DEC_BATCH, D_MODEL), lambda i: (layer, 0, 3)),
                  pl.BlockSpec((1, DEC_BATCH, D_MODEL), lambda i: (layer, 0, 4)),
                  pl.BlockSpec((1, D_MODEL), lambda i: (0, 0)),
                  pl.BlockSpec((1, D_MODEL, LANES), lambda i: (layer, 0, 0)),
                  pl.BlockSpec((1, 1, LANES), lambda i: (layer, 0, 0))],
        out_specs=(pl.BlockSpec((tm, D_MODEL // 2), tok),
                   pl.BlockSpec((tm, LANES), tok),
                   pl.BlockSpec((tm, LANES), tok),
                   pl.BlockSpec((tm, LANES), tok),
                   pl.BlockSpec((8, LANES), lambda i: (0, 0))),
        scratch_shapes=[pltpu.VMEM((8, LANES), F32)],
        compiler_params=_cparams(("arbitrary",)),
        name="ffn_pre",
    )(x_p, x_s, mod_p, mod_p, mod_s, mod_s, g, w_router_pad, b_router_pad)


def _route(idx, rank, counts):
    e = idx[:, :TOP_K]
    r = rank[:, :TOP_K]
    cnt = counts[0, :N_EXPERTS]
    ntile = (cnt + MOE_SUB - 1) // MOE_SUB
    tile_end = jnp.cumsum(ntile)
    tile_start = tile_end - ntile
    pos = (tile_start[e] * MOE_SUB + r).astype(I32)
    tok = jnp.broadcast_to(jnp.arange(T_ALL, dtype=I32)[:, None], pos.shape)
    row_tok = jnp.zeros((MOE_ROWS,), I32).at[pos.reshape(-1)].set(tok.reshape(-1))
    npass = (ntile + MOE_NSUB - 1) // MOE_NSUB
    item_end = jnp.cumsum(npass)
    item_start = item_end - npass
    n_items = item_end[-1]
    it = jnp.arange(MOE_ITEMS + 1, dtype=I32)
    it_c = jnp.minimum(it, n_items - 1)
    item_e = jnp.minimum(jnp.searchsorted(item_end, it_c, side="right"), N_EXPERTS - 1).astype(I32)
    p = it_c - item_start[item_e]
    item_row0 = ((tile_start[item_e] + p * MOE_NSUB) * MOE_SUB).astype(I32)
    item_nsub = jnp.where(it < n_items, jnp.minimum(ntile[item_e] - p * MOE_NSUB, MOE_NSUB), 0).astype(I32)
    return pos.reshape(-1), row_tok, item_e, item_row0, item_nsub, n_items.reshape(1).astype(I32)


def _experts_body(item_e, item_row0, item_nsub, n_items, row_tok,
                  hw_hbm, wg_ref, wu_ref, bg_ref, bu_ref, wd_ref, bd_ref,
                  ys_hbm,
                  xraw, hbuf, wgb, wub, wdb, ybuf, gsem, ysem, pend, pend_row, pend_col):
    it = pl.program_id(0)
    j = pl.program_id(1)
    nsub = item_nsub[it]
    row0 = item_row0[it]
    nsub_next = item_nsub[it + 1]
    row0_next = item_row0[it + 1]
    n_pref = jnp.where(it > 0, jnp.minimum(item_nsub[jnp.maximum(it - 1, 0)], nsub), 0)
    active = it < n_items[0]
    ts = MOE_SUB
    tn = MOE_TN
    pf_rows = ts // MOE_JD

    def chunk(c):
        return slice(c * tn, (c + 1) * tn)

    def row_copy(base, r, s):
        return pltpu.make_async_copy(hw_hbm.at[row_tok[base + r]], xraw.at[r], gsem.at[s])

    def out_copy(row, slot, col):
        dst = ys_hbm.at[pl.ds(pl.multiple_of(row, ts), ts), pl.ds(pl.multiple_of(col * tn, tn), tn)]
        return pltpu.make_async_copy(ybuf.at[slot], dst, ysem.at[slot])

    def out_wait(slot):
        @pl.when(pend[slot] == 1)
        def _():
            out_copy(pend_row[slot], slot, pend_col[slot]).wait()
            pend[slot] = 0

    def out_start(row, slot, col):
        out_copy(row, slot, col).start()
        pend[slot] = 1
        pend_row[slot] = row
        pend_col[slot] = col

    @pl.when(jnp.logical_and(it == 0, j == 0))
    def _():
        pend[0] = 0
        pend[1] = 0

    def swiglu(gt, up):
        gt = jnp.minimum(gt, SWIGLU_LIMIT)
        up = jnp.clip(up, -SWIGLU_LIMIT, SWIGLU_LIMIT)
        return ((up + 1.0) * gt * jax.nn.sigmoid(SWIGLU_ALPHA * gt)).astype(BF16)

    @pl.when(jnp.logical_and(active, j == 0))
    def _gather():
        def issue(r, c):
            row_copy(row0, r, lax.shift_right_logical(r, int(np.log2(ts)))).start()
            return c
        lax.fori_loop(n_pref * ts, nsub * ts, issue, 0)

        def land(s, c):
            rows = pl.ds(pl.multiple_of(s * ts, ts), ts)
            pltpu.make_async_copy(hw_hbm.at[pl.ds(0, ts)], xraw.at[rows], gsem.at[s]).wait()
            return c
        lax.fori_loop(0, nsub, land, 0)

    def unpack(w, high):
        bits = (w & jnp.uint32(0xFFFF0000)) if high else (w << 16)
        return lax.bitcast_convert_type(bits, F32).astype(BF16)

    @pl.when(jnp.logical_and(active, j < MOE_JGU))
    def _gate_up():
        per_half = (D_MODEL // 2) // tn
        gt = jnp.broadcast_to(bg_ref[0, 0], (ts, tn))
        up = jnp.broadcast_to(bu_ref[0, 0], (ts, tn))
        for c in range(D_MODEL // tn):
            wg_c = wg_ref[0, 0, chunk(c), :].astype(BF16)
            wu_c = wu_ref[0, 0, chunk(c), :].astype(BF16)
            wgb[chunk(c), :] = wg_c
            wub[chunk(c), :] = wu_c
            x_c = unpack(xraw[0:ts, chunk(c % per_half)], c >= per_half)
            gt = gt + _bdot(x_c, wg_c)
            up = up + _bdot(x_c, wu_c)
        hbuf[j, 0:ts, :] = swiglu(gt, up)

        def sub(s, c):
            rows = pl.ds(pl.multiple_of(s * ts, ts), ts)
            w = xraw[rows, :]
            x = jnp.concatenate([unpack(w, False), unpack(w, True)], axis=1)
            hbuf[j, rows, :] = swiglu(_bdot(x, wgb[...]) + bg_ref[0, 0], _bdot(x, wub[...]) + bu_ref[0, 0])
            return c
        lax.fori_loop(1, nsub, sub, 0)

    @pl.when(jnp.logical_and(active, j >= MOE_JGU))
    def _down():
        jd = j - MOE_JGU
        n_both = jnp.minimum(nsub, nsub_next)

        def down_tile(s, first, prefetch):
            rows = slice(0, ts) if first else pl.ds(pl.multiple_of(s * ts, ts), ts)
            slot = 0 if first else s % 2
            out_wait(slot)
            acc = jnp.broadcast_to(bd_ref[0, 0], (ts, tn))
            for c in range(MOE_JGU):
                if first:
                    wd_c = wd_ref[0, 0, chunk(c), :].astype(BF16)
                    wdb[chunk(c), :] = wd_c
                else:
                    wd_c = wdb[chunk(c), :]
                acc = acc + _bdot(hbuf[c, rows, :], wd_c)
            ybuf[slot] = acc
            out_start(row0 + s * ts, slot, jd)
            if prefetch:
                for q in range(pf_rows):
                    row_copy(row0_next, s * ts + jd * pf_rows + q, s).start()

        @pl.when(n_both >= 1)
        def _():
            down_tile(0, True, True)

        @pl.when(n_both < 1)
        def _():
            down_tile(0, True, False)

        def sub_pf(s, c):
            down_tile(s, False, True)
            return c
        lax.fori_loop(1, n_both, sub_pf, 0)

        def sub(s, c):
            down_tile(s, False, False)
            return c
        lax.fori_loop(jnp.maximum(n_both, 1), nsub, sub, 0)

    @pl.when(jnp.logical_and(it == pl.num_programs(0) - 1, j == pl.num_programs(1) - 1))
    def _fill_tail():
        out_wait(0)
        out_wait(1)
        ybuf[0] = jnp.zeros((ts, tn), F32)
        last = n_items[0] - 1
        used = item_row0[last] // ts + item_nsub[last]

        def tail_copy(t, c):
            dst = ys_hbm.at[pl.ds(pl.multiple_of(t * ts, ts), ts), pl.ds(c * tn, tn)]
            return pltpu.make_async_copy(ybuf.at[0], dst, ysem.at[0])

        def fill(t, carry):
            for c in range(MOE_JD):
                tail_copy(t, c).start()
            for c in range(MOE_JD):
                tail_copy(t, c).wait()
            return carry
        lax.fori_loop(used, MOE_ROW_TILES, fill, 0)


def _experts(route, hw, w_gate_up, b_gate_up, w_down, b_down, layer):
    _, row_tok, item_e, item_row0, item_nsub, n_items = route
    tn = MOE_TN
    jgu = MOE_JGU
    gu_c = lambda j: jnp.minimum(j, jgu - 1)
    dn_c = lambda j: jnp.maximum(j - jgu, 0)
    grid_spec = pltpu.PrefetchScalarGridSpec(
        num_scalar_prefetch=5,
        grid=(MOE_ITEMS, MOE_JGU + MOE_JD),
        in_specs=[
            pl.BlockSpec(memory_space=pl.ANY),
            pl.BlockSpec((1, 1, D_MODEL, tn), lambda it, j, ie, *_: (layer, ie[it], 0, gu_c(j))),
            pl.BlockSpec((1, 1, D_MODEL, tn), lambda it, j, ie, *_: (layer, ie[it], 0, jgu + gu_c(j))),
            pl.BlockSpec((1, 1, 1, tn), lambda it, j, ie, *_: (layer, ie[it], 0, gu_c(j))),
            pl.BlockSpec((1, 1, 1, tn), lambda it, j, ie, *_: (layer, ie[it], 0, jgu + gu_c(j))),
            pl.BlockSpec((1, 1, D_FF, tn), lambda it, j, ie, *_: (layer, ie[it], 0, dn_c(j))),
            pl.BlockSpec((1, 1, 1, tn), lambda it, j, ie, *_: (layer, ie[it], 0, dn_c(j))),
        ],
        out_specs=pl.BlockSpec(memory_space=pl.ANY),
        scratch_shapes=[
            pltpu.VMEM((MOE_NSUB * MOE_SUB, D_MODEL // 2), U32),
            pltpu.VMEM((MOE_JGU, MOE_NSUB * MOE_SUB, tn), BF16),
            pltpu.VMEM((D_MODEL, tn), BF16),
            pltpu.VMEM((D_MODEL, tn), BF16),
            pltpu.VMEM((D_FF, tn), BF16),
            pltpu.VMEM((2, MOE_SUB, tn), F32),
            pltpu.SemaphoreType.DMA((MOE_NSUB,)),
            pltpu.SemaphoreType.DMA((2,)),
            pltpu.SMEM((2,), I32),
            pltpu.SMEM((2,), I32),
            pltpu.SMEM((2,), I32),
        ],
    )
    ne = w_gate_up.shape[1]
    return pl.pallas_call(
        _experts_body,
        out_shape=jax.ShapeDtypeStruct((MOE_ROWS, D_MODEL), F32),
        grid_spec=grid_spec,
        compiler_params=_cparams(("arbitrary", "arbitrary")),
        name="moe_experts",
    )(item_e, item_row0, item_nsub, n_items, row_tok,
      hw, w_gate_up, w_gate_up,
      b_gate_up.reshape(DEPTH, ne, 1, 2 * D_FF), b_gate_up.reshape(DEPTH, ne, 1, 2 * D_FF),
      w_down, b_down.reshape(DEPTH, ne, 1, D_MODEL))


def _combine_body(pos, ys_hbm, xp_ref, xs_ref, gate_ref, gtp, gts, o_ref, ybuf, sem):
    i = pl.program_id(0)
    tm = TOK_TILE

    def issue_tile(tile, slot):
        base = tile * (tm * TOP_K)

        def issue(t, c):
            for kk in range(TOP_K):
                pltpu.make_async_copy(ys_hbm.at[pos[base + t * TOP_K + kk]], ybuf.at[slot, kk, t],
                                      sem.at[slot]).start()
            return c
        lax.fori_loop(0, tm, issue, 0)

    @pl.when(i == 0)
    def _():
        issue_tile(0, 0)

    @pl.when(i + 1 < pl.num_programs(0))
    def _():
        issue_tile(i + 1, (i + 1) % 2)

    slot = i % 2
    for kk in range(TOP_K):
        pltpu.make_async_copy(ys_hbm.at[pl.ds(0, tm)], ybuf.at[slot, kk], sem.at[slot]).wait()

    gates = gate_ref[...]
    f = gates[:, 0:1] * ybuf[slot, 0]
    for kk in range(1, TOP_K):
        f = f + gates[:, kk:kk + 1] * ybuf[slot, kk]
    x = jnp.where(i == N_PROMPT_TILES, xs_ref[...], xp_ref[...])
    o_ref[...] = x + _pick_mod(i, gtp, gts) * f


def _combine(pos, ys, x_p, x_s, gates, mod_p, mod_s, layer):
    tm = TOK_TILE
    grid_spec = pltpu.PrefetchScalarGridSpec(
        num_scalar_prefetch=1,
        grid=(N_TOK_TILES,),
        in_specs=[pl.BlockSpec(memory_space=pl.ANY),
                  pl.BlockSpec((tm, D_MODEL), lambda i, *_: (jnp.minimum(i, N_PROMPT_TILES - 1), 0)),
                  pl.BlockSpec((tm, D_MODEL), lambda i, *_: (0, 0)),
                  pl.BlockSpec((tm, LANES), lambda i, *_: (i, 0)),
                  pl.BlockSpec((1, 8, D_MODEL), lambda i, *_: (layer, 0, 5)),
                  pl.BlockSpec((1, DEC_BATCH, D_MODEL), lambda i, *_: (layer, 0, 5))],
        out_specs=pl.BlockSpec((tm, D_MODEL), lambda i, *_: (i, 0)),
        scratch_shapes=[pltpu.VMEM((2, TOP_K, tm, D_MODEL), F32), pltpu.SemaphoreType.DMA((2,))],
    )
    return pl.pallas_call(
        _combine_body,
        out_shape=jax.ShapeDtypeStruct((T_ALL, D_MODEL), F32),
        grid_spec=grid_spec,
        compiler_params=_cparams(("arbitrary",)),
        name="moe_combine",
    )(pos, ys, x_p, x_s, gates, mod_p, mod_s)


def _final_norm_body(x_ref, g_ref, op_ref, os_ref):
    i = pl.program_id(0)
    x = x_ref[...]
    y = x * lax.rsqrt(jnp.mean(x * x, axis=-1, keepdims=True) + NORM_EPS) * g_ref[...]

    @pl.when(i < N_PROMPT_TILES)
    def _():
        op_ref[...] = y

    @pl.when(i == N_PROMPT_TILES)
    def _():
        os_ref[...] = y


def _final_norm(x_all, g):
    tm = TOK_TILE
    return pl.pallas_call(
        _final_norm_body,
        out_shape=(jax.ShapeDtypeStruct((TP, D_MODEL), F32),
                   jax.ShapeDtypeStruct((DEC_BATCH, D_MODEL), F32)),
        grid=(N_TOK_TILES,),
        in_specs=[pl.BlockSpec((tm, D_MODEL), lambda i: (i, 0)),
                  pl.BlockSpec((1, D_MODEL), lambda i: (0, 0))],
        out_specs=(pl.BlockSpec((tm, D_MODEL), lambda i: (jnp.minimum(i, N_PROMPT_TILES - 1), 0)),
                   pl.BlockSpec((tm, D_MODEL), lambda i: (0, 0))),
        compiler_params=_cparams(("arbitrary",)),
        name="final_norm",
    )(x_all, g)


def _rope_tables(pos):
    inv = ROPE_BASE ** (-jnp.linspace(0.0, 1.0, RET_DK // 2, dtype=F32))
    ang = pos.astype(F32)[:, None] * inv[None, :]
    return jnp.cos(ang), jnp.sin(ang)


def kernel(x_prompt, x_sample, state_pool, state_ret, c_prompt, c_sample, w_mod, b_mod, g_norm_mix, g_norm_ffn,
           pool_w, pool_scale, ret_w_in, ret_w_out, moe_w_router, moe_b_router, moe_w_gate_up, moe_b_gate_up,
           moe_w_down, moe_b_down, g_final):
    assert x_prompt.shape == (BATCH, SEQ, D_MODEL) and x_sample.shape == (DEC_BATCH, 1, D_MODEL)
    c_prompt8 = jnp.concatenate([c_prompt, jnp.zeros((8 - BATCH, D_MODEL), F32)], axis=0)
    mod_s, mod_p = _adaln(c_sample, c_prompt8, w_mod, b_mod)
    cos_p, sin_p = _rope_tables(jnp.arange(SEQ))
    cos_s, sin_s = _rope_tables(PAST_LEN + jnp.arange(1))
    w_router_pad = jnp.pad(moe_w_router, ((0, 0), (0, 0), (0, LANES - N_EXPERTS)))
    b_router_pad = jnp.pad(moe_b_router, ((0, 0), (0, LANES - N_EXPERTS))).reshape(DEPTH, 1, LANES)

    x_p_src, x_p_tile0 = x_prompt.reshape(TP, D_MODEL), 0
    x_s_src, x_s_rows0 = x_sample.reshape(DEC_BATCH, D_MODEL), 0
    x_all = None
    new_pool_p, new_pool_s, new_ret_p = [], [], []
    ret_s_out = None
    for layer in range(DEPTH):
        j = layer // N_MIXERS
        g_mix = g_norm_mix[layer].reshape(1, D_MODEL)
        g_ffn = g_norm_ffn[layer].reshape(1, D_MODEL)
        if layer % N_MIXERS == 0:
            scale = pool_scale[j].reshape(1, D_MODEL)
            x_p, pool_p = _pool_prompt(x_p_src, x_p_tile0, mod_p, layer, g_mix, pool_w, scale, j)
            x_s, pool_s = _pool_sample(x_s_src, x_s_rows0 // 16, state_pool, mod_s, layer, g_mix, pool_w, scale, j)
            new_pool_p.append(pool_p)
            new_pool_s.append(pool_s)
        else:
            h_all = _norm_mod_all(x_all, mod_p, mod_s, layer, g_mix)
            proj = _in_proj(h_all, ret_w_in, j)
            u_p, ret_p = _ret_prompt(proj, cos_p, sin_p)
            u_s, ret_s_out = _ret_sample(proj, cos_s, sin_s, state_ret, j, ret_s_out)
            new_ret_p.append(ret_p)
            x_p = _out_proj(u_p, ret_w_out, j, x_all, 0, mod_p, layer, per_row=False)
            x_s = _out_proj(u_s, ret_w_out, j, x_all, TP // DEC_BATCH, mod_s, layer, per_row=True)
        hw, idx, gates, rank, counts = _ffn_pre(x_p, x_s, mod_p, mod_s, layer, g_ffn, w_router_pad, b_router_pad)
        route = _route(idx, rank, counts)
        ys = _experts(route, hw, moe_w_gate_up, moe_b_gate_up, moe_w_down, moe_b_down, layer)
        x_all = _combine(route[0], ys, x_p, x_s, gates, mod_p, mod_s, layer)
        x_p_src, x_p_tile0 = x_all, 0
        x_s_src, x_s_rows0 = x_all, TP
    y_p, y_s = _final_norm(x_all, g_final.reshape(1, D_MODEL))
    return (y_p.reshape(BATCH, SEQ, D_MODEL), y_s.reshape(DEC_BATCH, 1, D_MODEL),
            jnp.stack(new_pool_p), jnp.stack(new_pool_s), jnp.stack(new_ret_p), ret_s_out)
```

```python
import functools

import jax
import jax.numpy as jnp
import numpy as np
from jax import lax
from jax.experimental import pallas as pl
from jax.experimental.pallas import tpu as pltpu

F32 = jnp.float32
BF16 = jnp.bfloat16
I32 = jnp.int32
U32 = jnp.uint32

D_MODEL = 2048
BATCH = 4
SEQ = 2048
DEPTH = 4
DEC_BATCH = 128
PAST_LEN = 16384
N_MIXERS = 2
POOL_WINDOWS = (2, 4, 8, 16)
POOL_GROUP_DIM = D_MODEL // len(POOL_WINDOWS)
POOL_BUF = max(POOL_WINDOWS) - 1
POOL_HALO = 16
RET_HEADS = 8
RET_DK = D_MODEL // RET_HEADS
RET_DV = 2 * D_MODEL // RET_HEADS
RET_CHUNK = 128
RET_HK = RET_HEADS * RET_DK
RET_HV = RET_HEADS * RET_DV
RET_IN_DIM = 2 * RET_HK + 2 * RET_HV
ROPE_BASE = 10000.0
N_EXPERTS = 32
TOP_K = 4
D_FF = D_MODEL
SWIGLU_LIMIT = 7.0
SWIGLU_ALPHA = 1.702
NORM_EPS = 1e-5

TP = BATCH * SEQ
T_ALL = TP + DEC_BATCH
TOK_TILE = 128
N_TOK_TILES = T_ALL // TOK_TILE
N_PROMPT_TILES = TP // TOK_TILE
TILES_PER_SEQ = SEQ // TOK_TILE
LANES = 128
PACK_ROWS = D_MODEL // 2 // LANES

MOE_SUB = 256
MOE_NSUB = 8
MOE_TN = 512
MOE_JGU = D_FF // MOE_TN
MOE_JD = D_MODEL // MOE_TN
N_ASSIGN = T_ALL * TOP_K
MOE_ROW_TILES = -(-(N_ASSIGN + N_EXPERTS * (MOE_SUB - 1)) // MOE_SUB)
MOE_ROWS = MOE_ROW_TILES * MOE_SUB
MOE_ITEMS = N_EXPERTS + MOE_ROW_TILES // MOE_NSUB

VMEM_LIMIT = 56 * 1024 * 1024


def _cparams(sem, vmem=VMEM_LIMIT):
    return pltpu.CompilerParams(dimension_semantics=sem, vmem_limit_bytes=vmem)


def _norm_mod(x, g, shift, scale):
    ms = jnp.mean(x * x, axis=-1, keepdims=True)
    y = x * lax.rsqrt(ms + NORM_EPS)
    return (y * g) * (1.0 + scale) + shift


def _silu(x):
    return x * jax.nn.sigmoid(x)


def _bdot(a, b):
    return jnp.dot(a, b, preferred_element_type=F32)


def _adaln_body(cs_ref, cp_ref, w_ref, b_ref, os_ref, op_ref):
    w = w_ref[0].astype(BF16)
    b = b_ref[0]
    os_ref[0] = _bdot(_silu(cs_ref[...]).astype(BF16), w) + b
    op_ref[0] = _bdot(_silu(cp_ref[...]).astype(BF16), w) + b


def _adaln(c_sample, c_prompt8, w_mod, b_mod):
    nl, d, n6 = w_mod.shape
    tn = 1024
    return pl.pallas_call(
        _adaln_body,
        out_shape=(jax.ShapeDtypeStruct((nl, DEC_BATCH, n6), F32),
                   jax.ShapeDtypeStruct((nl, 8, n6), F32)),
        grid=(nl, n6 // tn),
        in_specs=[pl.BlockSpec((DEC_BATCH, d), lambda l, n: (0, 0)),
                  pl.BlockSpec((8, d), lambda l, n: (0, 0)),
                  pl.BlockSpec((1, d, tn), lambda l, n: (l, 0, n)),
                  pl.BlockSpec((1, 1, tn), lambda l, n: (l, 0, n))],
        out_specs=(pl.BlockSpec((1, DEC_BATCH, tn), lambda l, n: (l, 0, n)),
                   pl.BlockSpec((1, 8, tn), lambda l, n: (l, 0, n))),
        compiler_params=_cparams(("arbitrary", "arbitrary")),
        name="adaln",
    )(c_sample, c_prompt8, w_mod, b_mod.reshape(nl, 1, n6))


def _pick_mod(i, mp_ref, ms_ref):
    b = jnp.minimum(i // TILES_PER_SEQ, BATCH - 1)
    return jnp.where(i == N_PROMPT_TILES, ms_ref[0], mp_ref[0, pl.ds(b, 1), :])


def _pool_group_dots(diff, pw_ref):
    ys = []
    for gi in range(len(POOL_WINDOWS)):
        cols = slice(gi * POOL_GROUP_DIM, (gi + 1) * POOL_GROUP_DIM)
        ys.append(_bdot(diff[gi] if isinstance(diff, (list, tuple)) else diff[:, cols],
                        pw_ref[0, gi].astype(BF16)))
    return jnp.concatenate(ys, axis=1)


def _pool_prompt_body(x_ref, sh_ref, sc_ref, gt_ref, g_ref, pw_ref, ps_ref, xo_ref, np_ref, hbuf, *, ts):
    b = pl.program_id(0)
    s = pl.program_id(1)

    @pl.when(s == 0)
    def _():
        hbuf[0:POOL_HALO, :] = jnp.zeros((POOL_HALO, D_MODEL), F32)

    x = x_ref[...]
    shift = sh_ref[0, pl.ds(b, 1), :]
    scale = sc_ref[0, pl.ds(b, 1), :]
    gate = gt_ref[0, pl.ds(b, 1), :]
    h = _norm_mod(x, g_ref[...], shift, scale)
    hbuf[POOL_HALO:, :] = h
    t = s * ts + lax.broadcasted_iota(I32, (ts, 1), 0)
    diffs = []
    for gi, w in enumerate(POOL_WINDOWS):
        cols = slice(gi * POOL_GROUP_DIM, (gi + 1) * POOL_GROUP_DIM)
        acc = hbuf[:, cols]
        k = 1
        while k < w:
            acc = acc + pltpu.roll(acc, k, 0)
            k *= 2
        cnt = jnp.minimum(t + 1, w).astype(F32)
        diffs.append((acc[POOL_HALO:, :] / cnt - h[:, cols]).astype(BF16))
    y = _pool_group_dots(diffs, pw_ref) * ps_ref[...]
    xo_ref[...] = x + gate * y

    @pl.when(s == pl.num_programs(1) - 1)
    def _():
        np_ref[0] = hbuf[pl.ds(POOL_HALO + ts - POOL_BUF, POOL_BUF), :]

    hbuf[0:POOL_HALO, :] = hbuf[ts:ts + POOL_HALO, :]


def _pool_prompt(x2d, row_tile0, mod_p, layer, g, pool_w, pool_scale, j):
    ts = 256
    ns = SEQ // ts
    mspec = lambda k: pl.BlockSpec((1, 8, D_MODEL), lambda b, s: (layer, 0, k))
    return pl.pallas_call(
        functools.partial(_pool_prompt_body, ts=ts),
        out_shape=(jax.ShapeDtypeStruct((TP, D_MODEL), F32),
                   jax.ShapeDtypeStruct((BATCH, POOL_BUF, D_MODEL), F32)),
        grid=(BATCH, ns),
        in_specs=[pl.BlockSpec((ts, D_MODEL), lambda b, s: (row_tile0 + b * ns + s, 0)),
                  mspec(0), mspec(1), mspec(2),
                  pl.BlockSpec((1, D_MODEL), lambda b, s: (0, 0)),
                  pl.BlockSpec((1, len(POOL_WINDOWS), POOL_GROUP_DIM, POOL_GROUP_DIM),
                               lambda b, s: (j, 0, 0, 0)),
                  pl.BlockSpec((1, D_MODEL), lambda b, s: (0, 0))],
        out_specs=(pl.BlockSpec((ts, D_MODEL), lambda b, s: (b * ns + s, 0)),
                   pl.BlockSpec((1, POOL_BUF, D_MODEL), lambda b, s: (b, 0, 0))),
        scratch_shapes=[pltpu.VMEM((POOL_HALO + ts, D_MODEL), F32)],
        compiler_params=_cparams(("arbitrary", "arbitrary")),
        name="pool_prompt",
    )(x2d, mod_p, mod_p, mod_p, g, pool_w, pool_scale)


def _pool_sample_body(x_ref, st_ref, sh_ref, sc_ref, gt_ref, g_ref, pw_ref, ps_ref, xo_ref, npo_ref, dbuf, *, sb):
    x = x_ref[...]
    h = _norm_mod(x, g_ref[...], sh_ref[0], sc_ref[0])
    row = lax.broadcasted_iota(I32, (POOL_BUF, 1), 0)
    for s in range(sb):
        st = st_ref[0, s]
        hs = h[s:s + 1, :]
        parts = []
        for gi, w in enumerate(POOL_WINDOWS):
            cols = slice(gi * POOL_GROUP_DIM, (gi + 1) * POOL_GROUP_DIM)
            tail = jnp.sum(jnp.where(row >= POOL_BUF - (w - 1), st[:, cols], 0.0), axis=0, keepdims=True)
            parts.append((tail + hs[:, cols]) / float(w) - hs[:, cols])
        dbuf[s:s + 1, :] = jnp.concatenate(parts, axis=1)
        npo_ref[s, 0:POOL_BUF - 1, :] = st[1:POOL_BUF, :]
        npo_ref[s, POOL_BUF - 1:POOL_BUF, :] = hs
    y = _pool_group_dots(dbuf[...].astype(BF16), pw_ref) * ps_ref[...]
    xo_ref[...] = x + gt_ref[0] * y


def _pool_sample(x2d, row_tile0, state_pool, mod_s, layer, g, pool_w, pool_scale, j):
    sb = 16
    mspec = lambda k: pl.BlockSpec((1, sb, D_MODEL), lambda i: (layer, i, k))
    return pl.pallas_call(
        functools.partial(_pool_sample_body, sb=sb),
        out_shape=(jax.ShapeDtypeStruct((DEC_BATCH, D_MODEL), F32),
                   jax.ShapeDtypeStruct((DEC_BATCH, POOL_BUF, D_MODEL), F32)),
        grid=(DEC_BATCH // sb,),
        in_specs=[pl.BlockSpec((sb, D_MODEL), lambda i: (row_tile0 + i, 0)),
                  pl.BlockSpec((1, sb, POOL_BUF, D_MODEL), lambda i: (j, i, 0, 0)),
                  mspec(0), mspec(1), mspec(2),
                  pl.BlockSpec((1, D_MODEL), lambda i: (0, 0)),
                  pl.BlockSpec((1, len(POOL_WINDOWS), POOL_GROUP_DIM, POOL_GROUP_DIM),
                               lambda i: (j, 0, 0, 0)),
                  pl.BlockSpec((1, D_MODEL), lambda i: (0, 0))],
        out_specs=(pl.BlockSpec((sb, D_MODEL), lambda i: (i, 0)),
                   pl.BlockSpec((sb, POOL_BUF, D_MODEL), lambda i: (i, 0, 0))),
        scratch_shapes=[pltpu.VMEM((sb, D_MODEL), F32)],
        compiler_params=_cparams(("arbitrary",)),
        name="pool_sample",
    )(x2d, state_pool, mod_s, mod_s, mod_s, g, pool_w, pool_scale)


def _norm_mod_body(x_ref, shp, scp, shs, scs, g_ref, o_ref):
    i = pl.program_id(0)
    h = _norm_mod(x_ref[...], g_ref[...], _pick_mod(i, shp, shs), _pick_mod(i, scp, scs))
    o_ref[...] = h.astype(BF16)


def _norm_mod_all(x_all, mod_p, mod_s, layer, g):
    return pl.pallas_call(
        _norm_mod_body,
        out_shape=jax.ShapeDtypeStruct((T_ALL, D_MODEL), BF16),
        grid=(N_TOK_TILES,),
        in_specs=[pl.BlockSpec((TOK_TILE, D_MODEL), lambda i: (i, 0)),
                  pl.BlockSpec((1, 8, D_MODEL), lambda i: (layer, 0, 0)),
                  pl.BlockSpec((1, 8, D_MODEL), lambda i: (layer, 0, 1)),
                  pl.BlockSpec((1, DEC_BATCH, D_MODEL), lambda i: (layer, 0, 0)),
                  pl.BlockSpec((1, DEC_BATCH, D_MODEL), lambda i: (layer, 0, 1)),
                  pl.BlockSpec((1, D_MODEL), lambda i: (0, 0))],
        out_specs=pl.BlockSpec((TOK_TILE, D_MODEL), lambda i: (i, 0)),
        compiler_params=_cparams(("arbitrary",)),
        name="ret_norm_mod",
    )(x_all, mod_p, mod_p, mod_s, mod_s, g)


def _in_proj_body(a_ref, w_ref, o_ref):
    o_ref[...] = _bdot(a_ref[...], w_ref[0].astype(BF16))


def _in_proj(h_all, w_in, j):
    tm = T_ALL // 4
    tn = 512
    return pl.pallas_call(
        _in_proj_body,
        out_shape=jax.ShapeDtypeStruct((T_ALL, RET_IN_DIM), F32),
        grid=(T_ALL // tm, RET_IN_DIM // tn),
        in_specs=[pl.BlockSpec((tm, D_MODEL), lambda m, n: (m, 0)),
                  pl.BlockSpec((1, D_MODEL, tn), lambda m, n: (j, 0, n))],
        out_specs=pl.BlockSpec((tm, tn), lambda m, n: (m, n)),
        compiler_params=_cparams(("arbitrary", "arbitrary")),
        name="ret_in_proj",
    )(h_all, w_in)


def _rope(x, cos, sin, scale=None):
    half = RET_DK // 2
    x1 = x[:, :half]
    x2 = x[:, half:]
    out = jnp.concatenate([x1 * cos - x2 * sin, x1 * sin + x2 * cos], axis=1)
    return out if scale is None else out * scale


def _log_decay(h, shape):
    hf = jnp.full(shape, h, I32).astype(F32)
    return jnp.log(1.0 - jnp.exp2(-5.0 - hf))


def _group_norm_gate(o, g):
    o = o * lax.rsqrt(jnp.mean(o * o, axis=-1, keepdims=True) + NORM_EPS)
    return _silu(g) * o


def _ret_prompt_body(q_ref, k_ref, v_ref, g_ref, cos_ref, sin_ref, u_ref, s_ref):
    c = pl.program_id(1)
    cn = RET_CHUNK

    @pl.when(c == 0)
    def _():
        s_ref[...] = jnp.zeros_like(s_ref)

    cos = cos_ref[...]
    sin = sin_ref[...]
    i_col = lax.broadcasted_iota(I32, (cn, 1), 0).astype(F32)
    j_row = lax.broadcasted_iota(I32, (1, cn), 1).astype(F32)
    rel = i_col - j_row
    for h in range(RET_HEADS):
        kcols = slice(h * RET_DK, (h + 1) * RET_DK)
        vcols = slice(h * RET_DV, (h + 1) * RET_DV)
        q = _rope(q_ref[:, kcols], cos, sin)
        k = _rope(k_ref[:, kcols], cos, sin, RET_DK ** -0.5)
        v = v_ref[:, vcols].astype(BF16)
        intra = jnp.where(rel >= 0, jnp.exp(_log_decay(h, (cn, cn)) * jnp.maximum(rel, 0.0)), 0.0)
        lg_col = _log_decay(h, (cn, 1))
        q_dec = jnp.exp(lg_col * (i_col + 1.0))
        k_dec = jnp.exp(lg_col * (cn - 1.0 - i_col))
        c_dec = jnp.exp(_log_decay(h, (1, 1)) * float(cn))

        state = s_ref[0, h]
        s = lax.dot_general(q.astype(BF16), k.astype(BF16), (((1,), (1,)), ((), ())),
                            preferred_element_type=F32) * intra
        o = _bdot(s.astype(BF16), v) + _bdot((q * q_dec).astype(BF16), state.astype(BF16))
        kd_t = (k * k_dec).T.astype(BF16)
        s_ref[0, h] = c_dec * state + _bdot(kd_t, v)
        u_ref[:, vcols] = _group_norm_gate(o, g_ref[:, vcols]).astype(BF16)


def _ret_prompt(proj, cos, sin):
    nc = SEQ // RET_CHUNK
    cn = RET_CHUNK
    row = lambda b, c: b * nc + c
    return pl.pallas_call(
        _ret_prompt_body,
        out_shape=(jax.ShapeDtypeStruct((TP, RET_HV), BF16),
                   jax.ShapeDtypeStruct((BATCH, RET_HEADS, RET_DK, RET_DV), F32)),
        grid=(BATCH, nc),
        in_specs=[pl.BlockSpec((cn, RET_HK), lambda b, c: (row(b, c), 0)),
                  pl.BlockSpec((cn, RET_HK), lambda b, c: (row(b, c), 1)),
                  pl.BlockSpec((cn, RET_HV), lambda b, c: (row(b, c), 1)),
                  pl.BlockSpec((cn, RET_HV), lambda b, c: (row(b, c), 2)),
                  pl.BlockSpec((cn, RET_DK // 2), lambda b, c: (c, 0)),
                  pl.BlockSpec((cn, RET_DK // 2), lambda b, c: (c, 0))],
        out_specs=(pl.BlockSpec((cn, RET_HV), lambda b, c: (row(b, c), 0)),
                   pl.BlockSpec((1, RET_HEADS, RET_DK, RET_DV), lambda b, c: (b, 0, 0, 0))),
        compiler_params=_cparams(("arbitrary", "arbitrary")),
        name="ret_prompt",
    )(proj, proj, proj, proj, cos, sin)


def _ret_sample_body(q_ref, k_ref, v_ref, g_ref, cos_ref, sin_ref, si_ref, *rest, sb, aliased):
    if aliased:
        rest = rest[1:]
    u_ref, so_ref, obuf = rest
    h = pl.program_id(1)
    cos = cos_ref[...]
    sin = sin_ref[...]
    gamma = jnp.exp(_log_decay(h, (1, 1)))
    q = _rope(q_ref[...], cos, sin)
    k = _rope(k_ref[...], cos, sin, RET_DK ** -0.5)
    v = v_ref[...]
    qk = jnp.sum(q * k, axis=1, keepdims=True)
    pad = jnp.zeros((LANES - sb, RET_DK), F32)
    q_t = jnp.concatenate([q * gamma, pad], axis=0).T
    k_t = jnp.concatenate([k, pad], axis=0).T
    for s in range(sb):
        state = si_ref[0, s, 0]
        vs = v[s:s + 1, :]
        o_state = jnp.sum(q_t[:, s:s + 1] * state, axis=0, keepdims=True)
        so_ref[0, s, 0] = gamma * state + k_t[:, s:s + 1] * vs
        obuf[s:s + 1, :] = qk[s:s + 1, :] * vs + o_state
    u_ref[...] = _group_norm_gate(obuf[...], g_ref[...])


def _ret_sample(proj, cos_s, sin_s, state_ret, j, prev_out):
    sb = 8
    r0 = TP // sb
    aliased = prev_out is not None
    in_specs = [pl.BlockSpec((sb, RET_DK), lambda i, h: (r0 + i, h)),
                pl.BlockSpec((sb, RET_DK), lambda i, h: (r0 + i, RET_HEADS + h)),
                pl.BlockSpec((sb, RET_DV), lambda i, h: (r0 + i, RET_HEADS + h)),
                pl.BlockSpec((sb, RET_DV), lambda i, h: (r0 + i, 2 * RET_HEADS + h)),
                pl.BlockSpec((1, RET_DK // 2), lambda i, h: (0, 0)),
                pl.BlockSpec((1, RET_DK // 2), lambda i, h: (0, 0)),
                pl.BlockSpec((1, sb, 1, RET_DK, RET_DV), lambda i, h: (j, i, h, 0, 0))]
    args = [proj, proj, proj, proj, cos_s, sin_s, state_ret]
    aliases = {}
    if aliased:
        in_specs.append(pl.BlockSpec(memory_space=pl.ANY))
        args.append(prev_out)
        aliases = {len(args) - 1: 1}
    return pl.pallas_call(
        functools.partial(_ret_sample_body, sb=sb, aliased=aliased),
        out_shape=(jax.ShapeDtypeStruct((DEC_BATCH, RET_HV), F32),
                   jax.ShapeDtypeStruct(state_ret.shape, F32)),
        grid=(DEC_BATCH // sb, RET_HEADS),
        in_specs=in_specs,
        out_specs=(pl.BlockSpec((sb, RET_DV), lambda i, h: (i, h)),
                   pl.BlockSpec((1, sb, 1, RET_DK, RET_DV), lambda i, h: (j, i, h, 0, 0))),
        scratch_shapes=[pltpu.VMEM((sb, RET_DV), F32)],
        input_output_aliases=aliases,
        compiler_params=_cparams(("arbitrary", "arbitrary")),
        name="ret_sample",
    )(*args)


def _out_proj_body(u_ref, w_ref, x_ref, gt_ref, o_ref, *, per_row, tm):
    y = _bdot(u_ref[...].astype(BF16), w_ref[0].astype(BF16))
    if per_row:
        gate = gt_ref[0]
    else:
        gate = gt_ref[0, pl.ds(pl.program_id(0) * tm // SEQ, 1), :]
    o_ref[...] = x_ref[...] + gate * y


def _out_proj(u, w_out, j, x_all, row_tile0, mod, layer, per_row):
    m = u.shape[0]
    tm = min(m, 1024)
    tn = 512
    gate_blk = 2 * (D_MODEL // tn)
    mrows = mod.shape[1]
    return pl.pallas_call(
        functools.partial(_out_proj_body, per_row=per_row, tm=tm),
        out_shape=jax.ShapeDtypeStruct((m, D_MODEL), F32),
        grid=(m // tm, D_MODEL // tn),
        in_specs=[pl.BlockSpec((tm, RET_HV), lambda i, n: (i, 0)),
                  pl.BlockSpec((1, RET_HV, tn), lambda i, n: (j, 0, n)),
                  pl.BlockSpec((tm, tn), lambda i, n: (row_tile0 + i, n)),
                  pl.BlockSpec((1, mrows, tn), lambda i, n: (layer, 0, gate_blk + n))],
        out_specs=pl.BlockSpec((tm, tn), lambda i, n: (i, n)),
        compiler_params=_cparams(("arbitrary", "arbitrary")),
        name="ret_out_proj",
    )(u, w_out, x_all, mod)


def _ffn_pre_body(xp_ref, xs_ref, shp, scp, shs, scs, g_ref, wr_ref, br_ref,
                  hw_ref, idx_ref, gate_ref, rank_ref, cnt_ref, carry):
    i = pl.program_id(0)
    tm = TOK_TILE

    @pl.when(i == 0)
    def _():
        carry[...] = jnp.zeros_like(carry)

    x = jnp.where(i == N_PROMPT_TILES, xs_ref[...], xp_ref[...])
    h = _norm_mod(x, g_ref[...], _pick_mod(i, shp, shs), _pick_mod(i, scp, scs))

    bits = lax.bitcast_convert_type(h.astype(BF16).astype(F32), U32)
    half = D_MODEL // 2
    words = (bits[:, :half] >> 16) | (bits[:, half:] & jnp.uint32(0xFFFF0000))
    for a in range(PACK_ROWS):
        hw_ref[pl.ds(a, tm, stride=PACK_ROWS), :] = words[:, a * LANES:(a + 1) * LANES]

    lane = lax.broadcasted_iota(I32, (tm, LANES), 1)
    lane_f = lane.astype(F32)
    logits = jnp.dot(h, wr_ref[0], preferred_element_type=F32, precision=lax.Precision.HIGHEST) + br_ref[0]
    logits = jnp.where(lane < N_EXPERTS, logits, -jnp.inf)

    hots, vals = [], []
    idx_out = jnp.zeros((tm, LANES), F32)
    for kk in range(TOP_K):
        m = jnp.max(logits, axis=1, keepdims=True)
        pick = jnp.min(jnp.where(logits == m, lane_f, float(LANES)), axis=1, keepdims=True)
        hot = lane_f == pick
        logits = jnp.where(hot, -jnp.inf, logits)
        hots.append(hot)
        vals.append(m)
        idx_out = jnp.where(lane == kk, pick, idx_out)
    exps = [jnp.exp(v - vals[0]) for v in vals]
    den = exps[0] + exps[1] + exps[2] + exps[3]
    gate_out = jnp.zeros((tm, LANES), F32)
    for kk in range(TOP_K):
        gate_out = jnp.where(lane == kk, exps[kk] / den, gate_out)

    cnt = jnp.zeros((tm, LANES), F32)
    for hot in hots:
        cnt = cnt + hot.astype(F32)
    r_i = lax.broadcasted_iota(I32, (tm, tm), 0)
    c_i = lax.broadcasted_iota(I32, (tm, tm), 1)
    lower = jnp.where(c_i < r_i, 1.0, 0.0).astype(BF16)
    base = _bdot(lower, cnt.astype(BF16)) + carry[0:1, :]
    rank_out = jnp.zeros((tm, LANES), F32)
    for kk in range(TOP_K):
        rk = jnp.sum(jnp.where(hots[kk], base, 0.0), axis=1, keepdims=True)
        rank_out = jnp.where(lane == kk, rk, rank_out)
    total = carry[0:1, :] + jnp.sum(cnt, axis=0, keepdims=True)
    carry[0:1, :] = total

    idx_ref[...] = idx_out.astype(I32)
    gate_ref[...] = gate_out
    rank_ref[...] = rank_out.astype(I32)

    @pl.when(i == pl.num_programs(0) - 1)
    def _():
        cnt_ref[...] = jnp.broadcast_to(total, cnt_ref.shape).astype(I32)


def _ffn_pre(x_p, x_s, mod_p, mod_s, layer, g, w_router_pad, b_router_pad):
    tm = TOK_TILE
    tok = lambda i: (i, 0)
    return pl.pallas_call(
        _ffn_pre_body,
        out_shape=(jax.ShapeDtypeStruct((T_ALL * PACK_ROWS, LANES), U32),
                   jax.ShapeDtypeStruct((T_ALL, LANES), I32),
                   jax.ShapeDtypeStruct((T_ALL, LANES), F32),
                   jax.ShapeDtypeStruct((T_ALL, LANES), I32),
                   jax.ShapeDtypeStruct((8, LANES), I32)),
        grid=(N_TOK_TILES,),
        in_specs=[pl.BlockSpec((tm, D_MODEL), lambda i: (jnp.minimum(i, N_PROMPT_TILES - 1), 0)),
                  pl.BlockSpec((tm, D_MODEL), lambda i: (0, 0)),
                  pl.BlockSpec((1, 8, D_MODEL), lambda i: (layer, 0, 3)),
                  pl.BlockSpec((1, 8, D_MODEL), lambda i: (layer, 0, 4)),
                  pl.BlockSpec((1, DEC_BATCH, D_MODEL), lambda i: (layer, 0, 3)),
                  pl.BlockSpec((1, DEC_BATCH, D_MODEL), lambda i: (layer, 0, 4)),
                  pl.BlockSpec((1, D_MODEL), lambda i: (0, 0)),
                  pl.BlockSpec((1, D_MODEL, LANES), lambda i: (layer, 0, 0)),
                  pl.BlockSpec((1, 1, LANES), lambda i: (layer, 0, 0))],
        out_specs=(pl.BlockSpec((tm * PACK_ROWS, LANES), tok),
                   pl.BlockSpec((tm, LANES), tok),
                   pl.BlockSpec((tm, LANES), tok),
                   pl.BlockSpec((tm, LANES), tok),
                   pl.BlockSpec((8, LANES), lambda i: (0, 0))),
        scratch_shapes=[pltpu.VMEM((8, LANES), F32)],
        compiler_params=_cparams(("arbitrary",)),
        name="ffn_pre",
    )(x_p, x_s, mod_p, mod_p, mod_s, mod_s, g, w_router_pad, b_router_pad)


def _route(idx, rank, counts):
    e = idx[:, :TOP_K]
    r = rank[:, :TOP_K]
    cnt = counts[0, :N_EXPERTS]
    ntile = (cnt + MOE_SUB - 1) // MOE_SUB
    tile_end = jnp.cumsum(ntile)
    tile_start = tile_end - ntile
    pos = (tile_start[e] * MOE_SUB + r).astype(I32)
    tok = jnp.broadcast_to(jnp.arange(T_ALL, dtype=I32)[:, None], pos.shape)
    row_tok = jnp.zeros((MOE_ROWS,), I32).at[pos.reshape(-1)].set(tok.reshape(-1))
    npass = (ntile + MOE_NSUB - 1) // MOE_NSUB
    item_end = jnp.cumsum(npass)
    item_start = item_end - npass
    n_items = item_end[-1]
    it = jnp.arange(MOE_ITEMS + 1, dtype=I32)
    it_c = jnp.minimum(it, n_items - 1)
    item_e = jnp.minimum(jnp.searchsorted(item_end, it_c, side="right"), N_EXPERTS - 1).astype(I32)
    p = it_c - item_start[item_e]
    item_row0 = ((tile_start[item_e] + p * MOE_NSUB) * MOE_SUB).astype(I32)
    item_nsub = jnp.where(it < n_items, jnp.minimum(ntile[item_e] - p * MOE_NSUB, MOE_NSUB), 0).astype(I32)
    return pos.reshape(-1), row_tok, item_e, item_row0, item_nsub, n_items.reshape(1).astype(I32)


def _experts_body(item_e, item_row0, item_nsub, n_items, row_tok,
                  hw_hbm, wg_ref, wu_ref, bg_ref, bu_ref, wd_ref, bd_ref,
                  ys_hbm,
                  xraw, hbuf, wgb, wub, wdb, ybuf, gsem, ysem, pend, pend_row, pend_col):
    it = pl.program_id(0)
    j = pl.program_id(1)
    nsub = item_nsub[it]
    row0 = item_row0[it]
    nsub_next = item_nsub[it + 1]
    row0_next = item_row0[it + 1]
    n_pref = jnp.where(it > 0, jnp.minimum(item_nsub[jnp.maximum(it - 1, 0)], nsub), 0)
    active = it < n_items[0]
    ts = MOE_SUB
    tn = MOE_TN
    pf_rows = ts // MOE_JD

    def chunk(c):
        return slice(c * tn, (c + 1) * tn)

    def row_copy(base, r, s):
        src = hw_hbm.at[pl.ds(pl.multiple_of(row_tok[base + r] * PACK_ROWS, PACK_ROWS), PACK_ROWS)]
        dst = xraw.at[pl.ds(pl.multiple_of(r * PACK_ROWS, PACK_ROWS), PACK_ROWS)]
        return pltpu.make_async_copy(src, dst, gsem.at[s])

    def out_copy(row, slot, col):
        dst = ys_hbm.at[pl.ds(pl.multiple_of(row, ts), ts), pl.ds(pl.multiple_of(col * tn, tn), tn)]
        return pltpu.make_async_copy(ybuf.at[slot], dst, ysem.at[slot])

    def out_wait(slot):
        @pl.when(pend[slot] == 1)
        def _():
            out_copy(pend_row[slot], slot, pend_col[slot]).wait()
            pend[slot] = 0

    def out_start(row, slot, col):
        out_copy(row, slot, col).start()
        pend[slot] = 1
        pend_row[slot] = row
        pend_col[slot] = col

    @pl.when(jnp.logical_and(it == 0, j == 0))
    def _():
        pend[0] = 0
        pend[1] = 0

    def swiglu(gt, up):
        gt = jnp.minimum(gt, SWIGLU_LIMIT)
        up = jnp.clip(up, -SWIGLU_LIMIT, SWIGLU_LIMIT)
        return ((up + 1.0) * gt * jax.nn.sigmoid(SWIGLU_ALPHA * gt)).astype(BF16)

    @pl.when(jnp.logical_and(active, j == 0))
    def _gather():
        def issue(r, c):
            row_copy(row0, r, lax.shift_right_logical(r, int(np.log2(ts)))).start()
            return c
        lax.fori_loop(n_pref * ts, nsub * ts, issue, 0)

        def land(s, c):
            rows = pl.ds(pl.multiple_of(s * ts * PACK_ROWS, ts * PACK_ROWS), ts * PACK_ROWS)
            pltpu.make_async_copy(hw_hbm.at[pl.ds(0, ts * PACK_ROWS)], xraw.at[rows], gsem.at[s]).wait()
            return c
        lax.fori_loop(0, nsub, land, 0)

    def unpack(w, high):
        bits = (w & jnp.uint32(0xFFFF0000)) if high else (w << 16)
        return lax.bitcast_convert_type(bits, F32).astype(BF16)

    def x_cols(r0, lane_blocks, high):
        parts = [unpack(xraw[pl.ds(r0 * PACK_ROWS + a, ts, stride=PACK_ROWS), :], high) for a in lane_blocks]
        return jnp.concatenate(parts, axis=1)

    @pl.when(jnp.logical_and(active, j < MOE_JGU))
    def _gate_up():
        per_half = (D_MODEL // 2) // tn
        gt = jnp.broadcast_to(bg_ref[0, 0], (ts, tn))
        up = jnp.broadcast_to(bu_ref[0, 0], (ts, tn))
        for c in range(D_MODEL // tn):
            wg_c = wg_ref[0, 0, chunk(c), :].astype(BF16)
            wu_c = wu_ref[0, 0, chunk(c), :].astype(BF16)
            wgb[chunk(c), :] = wg_c
            wub[chunk(c), :] = wu_c
            blocks = range((c % per_half) * (tn // LANES), (c % per_half + 1) * (tn // LANES))
            x_c = x_cols(0, blocks, c >= per_half)
            gt = gt + _bdot(x_c, wg_c)
            up = up + _bdot(x_c, wu_c)
        hbuf[j, 0:ts, :] = swiglu(gt, up)

        def sub(s, c):
            rows = pl.ds(pl.multiple_of(s * ts, ts), ts)
            r0 = pl.multiple_of(s * ts, ts)
            ws = [xraw[pl.ds(r0 * PACK_ROWS + a, ts, stride=PACK_ROWS), :] for a in range(PACK_ROWS)]
            x = jnp.concatenate([unpack(w, False) for w in ws] + [unpack(w, True) for w in ws], axis=1)
            hbuf[j, rows, :] = swiglu(_bdot(x, wgb[...]) + bg_ref[0, 0], _bdot(x, wub[...]) + bu_ref[0, 0])
            return c
        lax.fori_loop(1, nsub, sub, 0)

    @pl.when(jnp.logical_and(active, j >= MOE_JGU))
    def _down():
        jd = j - MOE_JGU
        n_both = jnp.minimum(nsub, nsub_next)

        def down_tile(s, first, prefetch):
            rows = slice(0, ts) if first else pl.ds(pl.multiple_of(s * ts, ts), ts)
            slot = 0 if first else s % 2
            out_wait(slot)
            acc = jnp.broadcast_to(bd_ref[0, 0], (ts, tn))
            for c in range(MOE_JGU):
                if first:
                    wd_c = wd_ref[0, 0, chunk(c), :].astype(BF16)
                    wdb[chunk(c), :] = wd_c
                else:
                    wd_c = wdb[chunk(c), :]
                acc = acc + _bdot(hbuf[c, rows, :], wd_c)
            ybuf[slot] = acc
            out_start(row0 + s * ts, slot, jd)
            if prefetch:
                for q in range(pf_rows):
                    row_copy(row0_next, s * ts + jd * pf_rows + q, s).start()

        @pl.when(n_both >= 1)
        def _():
            down_tile(0, True, True)

        @pl.when(n_both < 1)
        def _():
            down_tile(0, True, False)

        def sub_pf(s, c):
            down_tile(s, False, True)
            return c
        lax.fori_loop(1, n_both, sub_pf, 0)

        def sub(s, c):
            down_tile(s, False, False)
            return c
        lax.fori_loop(jnp.maximum(n_both, 1), nsub, sub, 0)

    @pl.when(jnp.logical_and(it == pl.num_programs(0) - 1, j == pl.num_programs(1) - 1))
    def _fill_tail():
        out_wait(0)
        out_wait(1)
        ybuf[0] = jnp.zeros((ts, tn), F32)
        last = n_items[0] - 1
        used = item_row0[last] // ts + item_nsub[last]

        def tail_copy(t, c):
            dst = ys_hbm.at[pl.ds(pl.multiple_of(t * ts, ts), ts), pl.ds(c * tn, tn)]
            return pltpu.make_async_copy(ybuf.at[0], dst, ysem.at[0])

        def fill(t, carry):
            for c in range(MOE_JD):
                tail_copy(t, c).start()
            for c in range(MOE_JD):
                tail_copy(t, c).wait()
            return carry
        lax.fori_loop(used, MOE_ROW_TILES, fill, 0)


def _experts(route, hw, w_gate_up, b_gate_up, w_down, b_down, layer):
    _, row_tok, item_e, item_row0, item_nsub, n_items = route
    tn = MOE_TN
    jgu = MOE_JGU
    gu_c = lambda j: jnp.minimum(j, jgu - 1)
    dn_c = lambda j: jnp.maximum(j - jgu, 0)
    grid_spec = pltpu.PrefetchScalarGridSpec(
        num_scalar_prefetch=5,
        grid=(MOE_ITEMS, MOE_JGU + MOE_JD),
        in_specs=[
            pl.BlockSpec(memory_space=pl.ANY),
            pl.BlockSpec((1, 1, D_MODEL, tn), lambda it, j, ie, *_: (layer, ie[it], 0, gu_c(j))),
            pl.BlockSpec((1, 1, D_MODEL, tn), lambda it, j, ie, *_: (layer, ie[it], 0, jgu + gu_c(j))),
            pl.BlockSpec((1, 1, 1, tn), lambda it, j, ie, *_: (layer, ie[it], 0, gu_c(j))),
            pl.BlockSpec((1, 1, 1, tn), lambda it, j, ie, *_: (layer, ie[it], 0, jgu + gu_c(j))),
            pl.BlockSpec((1, 1, D_FF, tn), lambda it, j, ie, *_: (layer, ie[it], 0, dn_c(j))),
            pl.BlockSpec((1, 1, 1, tn), lambda it, j, ie, *_: (layer, ie[it], 0, dn_c(j))),
        ],
        out_specs=pl.BlockSpec(memory_space=pl.ANY),
        scratch_shapes=[
            pltpu.VMEM((MOE_NSUB * MOE_SUB * PACK_ROWS, LANES), U32),
            pltpu.VMEM((MOE_JGU, MOE_NSUB * MOE_SUB, tn), BF16),
            pltpu.VMEM((D_MODEL, tn), BF16),
            pltpu.VMEM((D_MODEL, tn), BF16),
            pltpu.VMEM((D_FF, tn), BF16),
            pltpu.VMEM((2, MOE_SUB, tn), F32),
            pltpu.SemaphoreType.DMA((MOE_NSUB,)),
            pltpu.SemaphoreType.DMA((2,)),
            pltpu.SMEM((2,), I32),
            pltpu.SMEM((2,), I32),
            pltpu.SMEM((2,), I32),
        ],
    )
    ne = w_gate_up.shape[1]
    return pl.pallas_call(
        _experts_body,
        out_shape=jax.ShapeDtypeStruct((MOE_ROWS, D_MODEL), F32),
        grid_spec=grid_spec,
        compiler_params=_cparams(("arbitrary", "arbitrary")),
        name="moe_experts",
    )(item_e, item_row0, item_nsub, n_items, row_tok,
      hw, w_gate_up, w_gate_up,
      b_gate_up.reshape(DEPTH, ne, 1, 2 * D_FF), b_gate_up.reshape(DEPTH, ne, 1, 2 * D_FF),
      w_down, b_down.reshape(DEPTH, ne, 1, D_MODEL))


def _combine_body(pos, ys_hbm, xp_ref, xs_ref, gate_ref, gtp, gts, o_ref, ybuf, sem):
    i = pl.program_id(0)
    tm = TOK_TILE

    def issue_tile(tile, slot):
        base = tile * (tm * TOP_K)

        def issue(t, c):
            for kk in range(TOP_K):
                pltpu.make_async_copy(ys_hbm.at[pos[base + t * TOP_K + kk]], ybuf.at[slot, kk, t],
                                      sem.at[slot]).start()
            return c
        lax.fori_loop(0, tm, issue, 0)

    @pl.when(i == 0)
    def _():
        issue_tile(0, 0)

    @pl.when(i + 1 < pl.num_programs(0))
    def _():
        issue_tile(i + 1, (i + 1) % 2)

    slot = i % 2
    for kk in range(TOP_K):
        pltpu.make_async_copy(ys_hbm.at[pl.ds(0, tm)], ybuf.at[slot, kk], sem.at[slot]).wait()

    gates = gate_ref[...]
    f = gates[:, 0:1] * ybuf[slot, 0]
    for kk in range(1, TOP_K):
        f = f + gates[:, kk:kk + 1] * ybuf[slot, kk]
    x = jnp.where(i == N_PROMPT_TILES, xs_ref[...], xp_ref[...])
    o_ref[...] = x + _pick_mod(i, gtp, gts) * f


def _combine(pos, ys, x_p, x_s, gates, mod_p, mod_s, layer):
    tm = TOK_TILE
    grid_spec = pltpu.PrefetchScalarGridSpec(
        num_scalar_prefetch=1,
        grid=(N_TOK_TILES,),
        in_specs=[pl.BlockSpec(memory_space=pl.ANY),
                  pl.BlockSpec((tm, D_MODEL), lambda i, *_: (jnp.minimum(i, N_PROMPT_TILES - 1), 0)),
                  pl.BlockSpec((tm, D_MODEL), lambda i, *_: (0, 0)),
                  pl.BlockSpec((tm, LANES), lambda i, *_: (i, 0)),
                  pl.BlockSpec((1, 8, D_MODEL), lambda i, *_: (layer, 0, 5)),
                  pl.BlockSpec((1, DEC_BATCH, D_MODEL), lambda i, *_: (layer, 0, 5))],
        out_specs=pl.BlockSpec((tm, D_MODEL), lambda i, *_: (i, 0)),
        scratch_shapes=[pltpu.VMEM((2, TOP_K, tm, D_MODEL), F32), pltpu.SemaphoreType.DMA((2,))],
    )
    return pl.pallas_call(
        _combine_body,
        out_shape=jax.ShapeDtypeStruct((T_ALL, D_MODEL), F32),
        grid_spec=grid_spec,
        compiler_params=_cparams(("arbitrary",)),
        name="moe_combine",
    )(pos, ys, x_p, x_s, gates, mod_p, mod_s)


def _final_norm_body(x_ref, g_ref, op_ref, os_ref):
    i = pl.program_id(0)
    x = x_ref[...]
    y = x * lax.rsqrt(jnp.mean(x * x, axis=-1, keepdims=True) + NORM_EPS) * g_ref[...]

    @pl.when(i < N_PROMPT_TILES)
    def _():
        op_ref[...] = y

    @pl.when(i == N_PROMPT_TILES)
    def _():
        os_ref[...] = y


def _final_norm(x_all, g):
    tm = TOK_TILE
    return pl.pallas_call(
        _final_norm_body,
        out_shape=(jax.ShapeDtypeStruct((TP, D_MODEL), F32),
                   jax.ShapeDtypeStruct((DEC_BATCH, D_MODEL), F32)),
        grid=(N_TOK_TILES,),
        in_specs=[pl.BlockSpec((tm, D_MODEL), lambda i: (i, 0)),
                  pl.BlockSpec((1, D_MODEL), lambda i: (0, 0))],
        out_specs=(pl.BlockSpec((tm, D_MODEL), lambda i: (jnp.minimum(i, N_PROMPT_TILES - 1), 0)),
                   pl.BlockSpec((tm, D_MODEL), lambda i: (0, 0))),
        compiler_params=_cparams(("arbitrary",)),
        name="final_norm",
    )(x_all, g)


def _rope_tables(pos):
    inv = ROPE_BASE ** (-jnp.linspace(0.0, 1.0, RET_DK // 2, dtype=F32))
    ang = pos.astype(F32)[:, None] * inv[None, :]
    return jnp.cos(ang), jnp.sin(ang)


def kernel(x_prompt, x_sample, state_pool, state_ret, c_prompt, c_sample, w_mod, b_mod, g_norm_mix, g_norm_ffn,
           pool_w, pool_scale, ret_w_in, ret_w_out, moe_w_router, moe_b_router, moe_w_gate_up, moe_b_gate_up,
           moe_w_down, moe_b_down, g_final):
    assert x_prompt.shape == (BATCH, SEQ, D_MODEL) and x_sample.shape == (DEC_BATCH, 1, D_MODEL)
    c_prompt8 = jnp.concatenate([c_prompt, jnp.zeros((8 - BATCH, D_MODEL), F32)], axis=0)
    mod_s, mod_p = _adaln(c_sample, c_prompt8, w_mod, b_mod)
    cos_p, sin_p = _rope_tables(jnp.arange(SEQ))
    cos_s, sin_s = _rope_tables(PAST_LEN + jnp.arange(1))
    w_router_pad = jnp.pad(moe_w_router, ((0, 0), (0, 0), (0, LANES - N_EXPERTS)))
    b_router_pad = jnp.pad(moe_b_router, ((0, 0), (0, LANES - N_EXPERTS))).reshape(DEPTH, 1, LANES)

    x_p_src, x_p_tile0 = x_prompt.reshape(TP, D_MODEL), 0
    x_s_src, x_s_rows0 = x_sample.reshape(DEC_BATCH, D_MODEL), 0
    x_all = None
    new_pool_p, new_pool_s, new_ret_p = [], [], []
    ret_s_out = None
    for layer in range(DEPTH):
        j = layer // N_MIXERS
        g_mix = g_norm_mix[layer].reshape(1, D_MODEL)
        g_ffn = g_norm_ffn[layer].reshape(1, D_MODEL)
        if layer % N_MIXERS == 0:
            scale = pool_scale[j].reshape(1, D_MODEL)
            x_p, pool_p = _pool_prompt(x_p_src, x_p_tile0, mod_p, layer, g_mix, pool_w, scale, j)
            x_s, pool_s = _pool_sample(x_s_src, x_s_rows0 // 16, state_pool, mod_s, layer, g_mix, pool_w, scale, j)
            new_pool_p.append(pool_p)
            new_pool_s.append(pool_s)
        else:
            h_all = _norm_mod_all(x_all, mod_p, mod_s, layer, g_mix)
            proj = _in_proj(h_all, ret_w_in, j)
            u_p, ret_p = _ret_prompt(proj, cos_p, sin_p)
            u_s, ret_s_out = _ret_sample(proj, cos_s, sin_s, state_ret, j, ret_s_out)
            new_ret_p.append(ret_p)
            x_p = _out_proj(u_p, ret_w_out, j, x_all, 0, mod_p, layer, per_row=False)
            x_s = _out_proj(u_s, ret_w_out, j, x_all, TP // DEC_BATCH, mod_s, layer, per_row=True)
        hw, idx, gates, rank, counts = _ffn_pre(x_p, x_s, mod_p, mod_s, layer, g_ffn, w_router_pad, b_router_pad)
        route = _route(idx, rank, counts)
        ys = _experts(route, hw, moe_w_gate_up, moe_b_gate_up, moe_w_down, moe_b_down, layer)
        x_all = _combine(route[0], ys, x_p, x_s, gates, mod_p, mod_s, layer)
        x_p_src, x_p_tile0 = x_all, 0
        x_s_src, x_s_rows0 = x_all, TP
    y_p, y_s = _final_norm(x_all, g_final.reshape(1, D_MODEL))
    return (y_p.reshape(BATCH, SEQ, D_MODEL), y_s.reshape(DEC_BATCH, 1, D_MODEL),
            jnp.stack(new_pool_p), jnp.stack(new_pool_s), jnp.stack(new_ret_p), ret_s_out)
```

```python
import functools

import jax
import jax.numpy as jnp
import numpy as np
from jax import lax
from jax.experimental import pallas as pl
from jax.experimental.pallas import tpu as pltpu

F32 = jnp.float32
BF16 = jnp.bfloat16
I32 = jnp.int32
U32 = jnp.uint32

D_MODEL = 2048
BATCH = 4
SEQ = 2048
DEPTH = 4
DEC_BATCH = 128
PAST_LEN = 16384
N_MIXERS = 2
POOL_WINDOWS = (2, 4, 8, 16)
POOL_GROUP_DIM = D_MODEL // len(POOL_WINDOWS)
POOL_BUF = max(POOL_WINDOWS) - 1
POOL_HALO = 16
RET_HEADS = 8
RET_DK = D_MODEL // RET_HEADS
RET_DV = 2 * D_MODEL // RET_HEADS
RET_CHUNK = 128
RET_HK = RET_HEADS * RET_DK
RET_HV = RET_HEADS * RET_DV
RET_IN_DIM = 2 * RET_HK + 2 * RET_HV
ROPE_BASE = 10000.0
N_EXPERTS = 32
TOP_K = 4
D_FF = D_MODEL
SWIGLU_LIMIT = 7.0
SWIGLU_ALPHA = 1.702
NORM_EPS = 1e-5

TP = BATCH * SEQ
T_ALL = TP + DEC_BATCH
TOK_TILE = 128
N_TOK_TILES = T_ALL // TOK_TILE
N_PROMPT_TILES = TP // TOK_TILE
TILES_PER_SEQ = SEQ // TOK_TILE
LANES = 128
PACK_ROWS = D_MODEL // 2 // LANES

MOE_SUB = 256
MOE_NSUB = 8
MOE_TN = 512
MOE_JGU = D_FF // MOE_TN
MOE_JD = D_MODEL // MOE_TN
N_ASSIGN = T_ALL * TOP_K
MOE_ROW_TILES = -(-(N_ASSIGN + N_EXPERTS * (MOE_SUB - 1)) // MOE_SUB)
MOE_ROWS = MOE_ROW_TILES * MOE_SUB
MOE_ITEMS = N_EXPERTS + MOE_ROW_TILES // MOE_NSUB

VMEM_LIMIT = 56 * 1024 * 1024


def _cparams(sem, vmem=VMEM_LIMIT):
    return pltpu.CompilerParams(dimension_semantics=sem, vmem_limit_bytes=vmem)


def _norm_mod(x, g, shift, scale):
    ms = jnp.mean(x * x, axis=-1, keepdims=True)
    y = x * lax.rsqrt(ms + NORM_EPS)
    return (y * g) * (1.0 + scale) + shift


def _silu(x):
    return x * jax.nn.sigmoid(x)


def _bdot(a, b):
    return jnp.dot(a, b, preferred_element_type=F32)


def _adaln_body(cs_ref, cp_ref, w_ref, b_ref, os_ref, op_ref):
    w = w_ref[0].astype(BF16)
    b = b_ref[0]
    os_ref[0] = _bdot(_silu(cs_ref[...]).astype(BF16), w) + b
    op_ref[0] = _bdot(_silu(cp_ref[...]).astype(BF16), w) + b


def _adaln(c_sample, c_prompt8, w_mod, b_mod):
    nl, d, n6 = w_mod.shape
    tn = 1024
    return pl.pallas_call(
        _adaln_body,
        out_shape=(jax.ShapeDtypeStruct((nl, DEC_BATCH, n6), F32),
                   jax.ShapeDtypeStruct((nl, 8, n6), F32)),
        grid=(nl, n6 // tn),
        in_specs=[pl.BlockSpec((DEC_BATCH, d), lambda l, n: (0, 0)),
                  pl.BlockSpec((8, d), lambda l, n: (0, 0)),
                  pl.BlockSpec((1, d, tn), lambda l, n: (l, 0, n)),
                  pl.BlockSpec((1, 1, tn), lambda l, n: (l, 0, n))],
        out_specs=(pl.BlockSpec((1, DEC_BATCH, tn), lambda l, n: (l, 0, n)),
                   pl.BlockSpec((1, 8, tn), lambda l, n: (l, 0, n))),
        compiler_params=_cparams(("arbitrary", "arbitrary")),
        name="adaln",
    )(c_sample, c_prompt8, w_mod, b_mod.reshape(nl, 1, n6))


def _pick_mod(i, mp_ref, ms_ref):
    b = jnp.minimum(i // TILES_PER_SEQ, BATCH - 1)
    return jnp.where(i == N_PROMPT_TILES, ms_ref[0], mp_ref[0, pl.ds(b, 1), :])


def _pool_group_dots(diff, pw_ref):
    ys = []
    for gi in range(len(POOL_WINDOWS)):
        cols = slice(gi * POOL_GROUP_DIM, (gi + 1) * POOL_GROUP_DIM)
        ys.append(_bdot(diff[gi] if isinstance(diff, (list, tuple)) else diff[:, cols],
                        pw_ref[0, gi].astype(BF16)))
    return jnp.concatenate(ys, axis=1)


def _pool_prompt_body(x_ref, sh_ref, sc_ref, gt_ref, g_ref, pw_ref, ps_ref, xo_ref, np_ref, hbuf, *, ts):
    b = pl.program_id(0)
    s = pl.program_id(1)

    @pl.when(s == 0)
    def _():
        hbuf[0:POOL_HALO, :] = jnp.zeros((POOL_HALO, D_MODEL), F32)

    x = x_ref[...]
    shift = sh_ref[0, pl.ds(b, 1), :]
    scale = sc_ref[0, pl.ds(b, 1), :]
    gate = gt_ref[0, pl.ds(b, 1), :]
    h = _norm_mod(x, g_ref[...], shift, scale)
    hbuf[POOL_HALO:, :] = h
    t = s * ts + lax.broadcasted_iota(I32, (ts, 1), 0)
    diffs = []
    for gi, w in enumerate(POOL_WINDOWS):
        cols = slice(gi * POOL_GROUP_DIM, (gi + 1) * POOL_GROUP_DIM)
        acc = hbuf[:, cols]
        k = 1
        while k < w:
            acc = acc + pltpu.roll(acc, k, 0)
            k *= 2
        cnt = jnp.minimum(t + 1, w).astype(F32)
        diffs.append((acc[POOL_HALO:, :] / cnt - h[:, cols]).astype(BF16))
    y = _pool_group_dots(diffs, pw_ref) * ps_ref[...]
    xo_ref[...] = x + gate * y

    @pl.when(s == pl.num_programs(1) - 1)
    def _():
        np_ref[0] = hbuf[pl.ds(POOL_HALO + ts - POOL_BUF, POOL_BUF), :]

    hbuf[0:POOL_HALO, :] = hbuf[ts:ts + POOL_HALO, :]


def _pool_prompt(x2d, row_tile0, mod_p, layer, g, pool_w, pool_scale, j):
    ts = 256
    ns = SEQ // ts
    mspec = lambda k: pl.BlockSpec((1, 8, D_MODEL), lambda b, s: (layer, 0, k))
    return pl.pallas_call(
        functools.partial(_pool_prompt_body, ts=ts),
        out_shape=(jax.ShapeDtypeStruct((TP, D_MODEL), F32),
                   jax.ShapeDtypeStruct((BATCH, POOL_BUF, D_MODEL), F32)),
        grid=(BATCH, ns),
        in_specs=[pl.BlockSpec((ts, D_MODEL), lambda b, s: (row_tile0 + b * ns + s, 0)),
                  mspec(0), mspec(1), mspec(2),
                  pl.BlockSpec((1, D_MODEL), lambda b, s: (0, 0)),
                  pl.BlockSpec((1, len(POOL_WINDOWS), POOL_GROUP_DIM, POOL_GROUP_DIM),
                               lambda b, s: (j, 0, 0, 0)),
                  pl.BlockSpec((1, D_MODEL), lambda b, s: (0, 0))],
        out_specs=(pl.BlockSpec((ts, D_MODEL), lambda b, s: (b * ns + s, 0)),
                   pl.BlockSpec((1, POOL_BUF, D_MODEL), lambda b, s: (b, 0, 0))),
        scratch_shapes=[pltpu.VMEM((POOL_HALO + ts, D_MODEL), F32)],
        compiler_params=_cparams(("arbitrary", "arbitrary")),
        name="pool_prompt",
    )(x2d, mod_p, mod_p, mod_p, g, pool_w, pool_scale)


def _pool_sample_body(x_ref, st_ref, sh_ref, sc_ref, gt_ref, g_ref, pw_ref, ps_ref, xo_ref, npo_ref, dbuf, *, sb):
    x = x_ref[...]
    h = _norm_mod(x, g_ref[...], sh_ref[0], sc_ref[0])
    row = lax.broadcasted_iota(I32, (POOL_BUF, 1), 0)
    for s in range(sb):
        st = st_ref[0, s]
        hs = h[s:s + 1, :]
        parts = []
        for gi, w in enumerate(POOL_WINDOWS):
            cols = slice(gi * POOL_GROUP_DIM, (gi + 1) * POOL_GROUP_DIM)
            tail = jnp.sum(jnp.where(row >= POOL_BUF - (w - 1), st[:, cols], 0.0), axis=0, keepdims=True)
            parts.append((tail + hs[:, cols]) / float(w) - hs[:, cols])
        dbuf[s:s + 1, :] = jnp.concatenate(parts, axis=1)
        npo_ref[s, 0:POOL_BUF - 1, :] = st[1:POOL_BUF, :]
        npo_ref[s, POOL_BUF - 1:POOL_BUF, :] = hs
    y = _pool_group_dots(dbuf[...].astype(BF16), pw_ref) * ps_ref[...]
    xo_ref[...] = x + gt_ref[0] * y


def _pool_sample(x2d, row_tile0, state_pool, mod_s, layer, g, pool_w, pool_scale, j):
    sb = 16
    mspec = lambda k: pl.BlockSpec((1, sb, D_MODEL), lambda i: (layer, i, k))
    return pl.pallas_call(
        functools.partial(_pool_sample_body, sb=sb),
        out_shape=(jax.ShapeDtypeStruct((DEC_BATCH, D_MODEL), F32),
                   jax.ShapeDtypeStruct((DEC_BATCH, POOL_BUF, D_MODEL), F32)),
        grid=(DEC_BATCH // sb,),
        in_specs=[pl.BlockSpec((sb, D_MODEL), lambda i: (row_tile0 + i, 0)),
                  pl.BlockSpec((1, sb, POOL_BUF, D_MODEL), lambda i: (j, i, 0, 0)),
                  mspec(0), mspec(1), mspec(2),
                  pl.BlockSpec((1, D_MODEL), lambda i: (0, 0)),
                  pl.BlockSpec((1, len(POOL_WINDOWS), POOL_GROUP_DIM, POOL_GROUP_DIM),
                               lambda i: (j, 0, 0, 0)),
                  pl.BlockSpec((1, D_MODEL), lambda i: (0, 0))],
        out_specs=(pl.BlockSpec((sb, D_MODEL), lambda i: (i, 0)),
                   pl.BlockSpec((sb, POOL_BUF, D_MODEL), lambda i: (i, 0, 0))),
        scratch_shapes=[pltpu.VMEM((sb, D_MODEL), F32)],
        compiler_params=_cparams(("arbitrary",)),
        name="pool_sample",
    )(x2d, state_pool, mod_s, mod_s, mod_s, g, pool_w, pool_scale)


def _norm_mod_body(x_ref, shp, scp, shs, scs, g_ref, o_ref):
    i = pl.program_id(0)
    h = _norm_mod(x_ref[...], g_ref[...], _pick_mod(i, shp, shs), _pick_mod(i, scp, scs))
    o_ref[...] = h.astype(BF16)


def _norm_mod_all(x_all, mod_p, mod_s, layer, g):
    return pl.pallas_call(
        _norm_mod_body,
        out_shape=jax.ShapeDtypeStruct((T_ALL, D_MODEL), BF16),
        grid=(N_TOK_TILES,),
        in_specs=[pl.BlockSpec((TOK_TILE, D_MODEL), lambda i: (i, 0)),
                  pl.BlockSpec((1, 8, D_MODEL), lambda i: (layer, 0, 0)),
                  pl.BlockSpec((1, 8, D_MODEL), lambda i: (layer, 0, 1)),
                  pl.BlockSpec((1, DEC_BATCH, D_MODEL), lambda i: (layer, 0, 0)),
                  pl.BlockSpec((1, DEC_BATCH, D_MODEL), lambda i: (layer, 0, 1)),
                  pl.BlockSpec((1, D_MODEL), lambda i: (0, 0))],
        out_specs=pl.BlockSpec((TOK_TILE, D_MODEL), lambda i: (i, 0)),
        compiler_params=_cparams(("arbitrary",)),
        name="ret_norm_mod",
    )(x_all, mod_p, mod_p, mod_s, mod_s, g)


def _in_proj_body(a_ref, w_ref, o_ref):
    o_ref[...] = _bdot(a_ref[...], w_ref[0].astype(BF16))


def _in_proj(h_all, w_in, j):
    tm = T_ALL // 4
    tn = 512
    return pl.pallas_call(
        _in_proj_body,
        out_shape=jax.ShapeDtypeStruct((T_ALL, RET_IN_DIM), F32),
        grid=(T_ALL // tm, RET_IN_DIM // tn),
        in_specs=[pl.BlockSpec((tm, D_MODEL), lambda m, n: (m, 0)),
                  pl.BlockSpec((1, D_MODEL, tn), lambda m, n: (j, 0, n))],
        out_specs=pl.BlockSpec((tm, tn), lambda m, n: (m, n)),
        compiler_params=_cparams(("arbitrary", "arbitrary")),
        name="ret_in_proj",
    )(h_all, w_in)


def _rope(x, cos, sin, scale=None):
    half = RET_DK // 2
    x1 = x[:, :half]
    x2 = x[:, half:]
    out = jnp.concatenate([x1 * cos - x2 * sin, x1 * sin + x2 * cos], axis=1)
    return out if scale is None else out * scale


def _log_decay(h, shape):
    hf = jnp.full(shape, h, I32).astype(F32)
    return jnp.log(1.0 - jnp.exp2(-5.0 - hf))


def _group_norm_gate(o, g):
    o = o * lax.rsqrt(jnp.mean(o * o, axis=-1, keepdims=True) + NORM_EPS)
    return _silu(g) * o


def _ret_prompt_body(q_ref, k_ref, v_ref, g_ref, cos_ref, sin_ref, u_ref, s_ref):
    c = pl.program_id(1)
    cn = RET_CHUNK

    @pl.when(c == 0)
    def _():
        s_ref[...] = jnp.zeros_like(s_ref)

    cos = cos_ref[...]
    sin = sin_ref[...]
    i_col = lax.broadcasted_iota(I32, (cn, 1), 0).astype(F32)
    j_row = lax.broadcasted_iota(I32, (1, cn), 1).astype(F32)
    rel = i_col - j_row
    for h in range(RET_HEADS):
        kcols = slice(h * RET_DK, (h + 1) * RET_DK)
        vcols = slice(h * RET_DV, (h + 1) * RET_DV)
        q = _rope(q_ref[:, kcols], cos, sin)
        k = _rope(k_ref[:, kcols], cos, sin, RET_DK ** -0.5)
        v = v_ref[:, vcols].astype(BF16)
        intra = jnp.where(rel >= 0, jnp.exp(_log_decay(h, (cn, cn)) * jnp.maximum(rel, 0.0)), 0.0)
        lg_col = _log_decay(h, (cn, 1))
        q_dec = jnp.exp(lg_col * (i_col + 1.0))
        k_dec = jnp.exp(lg_col * (cn - 1.0 - i_col))
        c_dec = jnp.exp(_log_decay(h, (1, 1)) * float(cn))

        state = s_ref[0, h]
        s = lax.dot_general(q.astype(BF16), k.astype(BF16), (((1,), (1,)), ((), ())),
                            preferred_element_type=F32) * intra
        o = _bdot(s.astype(BF16), v) + _bdot((q * q_dec).astype(BF16), state.astype(BF16))
        kd_t = (k * k_dec).T.astype(BF16)
        s_ref[0, h] = c_dec * state + _bdot(kd_t, v)
        u_ref[:, vcols] = _group_norm_gate(o, g_ref[:, vcols]).astype(BF16)


def _ret_prompt(proj, cos, sin):
    nc = SEQ // RET_CHUNK
    cn = RET_CHUNK
    row = lambda b, c: b * nc + c
    return pl.pallas_call(
        _ret_prompt_body,
        out_shape=(jax.ShapeDtypeStruct((TP, RET_HV), BF16),
                   jax.ShapeDtypeStruct((BATCH, RET_HEADS, RET_DK, RET_DV), F32)),
        grid=(BATCH, nc),
        in_specs=[pl.BlockSpec((cn, RET_HK), lambda b, c: (row(b, c), 0)),
                  pl.BlockSpec((cn, RET_HK), lambda b, c: (row(b, c), 1)),
                  pl.BlockSpec((cn, RET_HV), lambda b, c: (row(b, c), 1)),
                  pl.BlockSpec((cn, RET_HV), lambda b, c: (row(b, c), 2)),
                  pl.BlockSpec((cn, RET_DK // 2), lambda b, c: (c, 0)),
                  pl.BlockSpec((cn, RET_DK // 2), lambda b, c: (c, 0))],
        out_specs=(pl.BlockSpec((cn, RET_HV), lambda b, c: (row(b, c), 0)),
                   pl.BlockSpec((1, RET_HEADS, RET_DK, RET_DV), lambda b, c: (b, 0, 0, 0))),
        compiler_params=_cparams(("arbitrary", "arbitrary")),
        name="ret_prompt",
    )(proj, proj, proj, proj, cos, sin)


def _ret_sample_body(q_ref, k_ref, v_ref, g_ref, cos_ref, sin_ref, si_ref, *rest, sb, aliased):
    if aliased:
        rest = rest[1:]
    u_ref, so_ref, obuf = rest
    h = pl.program_id(1)
    cos = cos_ref[...]
    sin = sin_ref[...]
    gamma = jnp.exp(_log_decay(h, (1, 1)))
    q = _rope(q_ref[...], cos, sin)
    k = _rope(k_ref[...], cos, sin, RET_DK ** -0.5)
    v = v_ref[...]
    qk = jnp.sum(q * k, axis=1, keepdims=True)
    pad = jnp.zeros((LANES - sb, RET_DK), F32)
    q_t = jnp.concatenate([q * gamma, pad], axis=0).T
    k_t = jnp.concatenate([k, pad], axis=0).T
    for s in range(sb):
        state = si_ref[0, s, 0]
        vs = v[s:s + 1, :]
        o_state = jnp.sum(q_t[:, s:s + 1] * state, axis=0, keepdims=True)
        so_ref[0, s, 0] = gamma * state + k_t[:, s:s + 1] * vs
        obuf[s:s + 1, :] = qk[s:s + 1, :] * vs + o_state
    u_ref[...] = _group_norm_gate(obuf[...], g_ref[...])


def _ret_sample(proj, cos_s, sin_s, state_ret, j, prev_out):
    sb = 8
    r0 = TP // sb
    aliased = prev_out is not None
    in_specs = [pl.BlockSpec((sb, RET_DK), lambda i, h: (r0 + i, h)),
                pl.BlockSpec((sb, RET_DK), lambda i, h: (r0 + i, RET_HEADS + h)),
                pl.BlockSpec((sb, RET_DV), lambda i, h: (r0 + i, RET_HEADS + h)),
                pl.BlockSpec((sb, RET_DV), lambda i, h: (r0 + i, 2 * RET_HEADS + h)),
                pl.BlockSpec((1, RET_DK // 2), lambda i, h: (0, 0)),
                pl.BlockSpec((1, RET_DK // 2), lambda i, h: (0, 0)),
                pl.BlockSpec((1, sb, 1, RET_DK, RET_DV), lambda i, h: (j, i, h, 0, 0))]
    args = [proj, proj, proj, proj, cos_s, sin_s, state_ret]
    aliases = {}
    if aliased:
        in_specs.append(pl.BlockSpec(memory_space=pl.ANY))
        args.append(prev_out)
        aliases = {len(args) - 1: 1}
    return pl.pallas_call(
        functools.partial(_ret_sample_body, sb=sb, aliased=aliased),
        out_shape=(jax.ShapeDtypeStruct((DEC_BATCH, RET_HV), F32),
                   jax.ShapeDtypeStruct(state_ret.shape, F32)),
        grid=(DEC_BATCH // sb, RET_HEADS),
        in_specs=in_specs,
        out_specs=(pl.BlockSpec((sb, RET_DV), lambda i, h: (i, h)),
                   pl.BlockSpec((1, sb, 1, RET_DK, RET_DV), lambda i, h: (j, i, h, 0, 0))),
        scratch_shapes=[pltpu.VMEM((sb, RET_DV), F32)],
        input_output_aliases=aliases,
        compiler_params=_cparams(("arbitrary", "arbitrary")),
        name="ret_sample",
    )(*args)


def _out_proj_body(u_ref, w_ref, x_ref, gt_ref, o_ref, *, per_row, tm):
    y = _bdot(u_ref[...].astype(BF16), w_ref[0].astype(BF16))
    if per_row:
        gate = gt_ref[0]
    else:
        gate = gt_ref[0, pl.ds(pl.program_id(0) * tm // SEQ, 1), :]
    o_ref[...] = x_ref[...] + gate * y


def _out_proj(u, w_out, j, x_all, row_tile0, mod, layer, per_row):
    m = u.shape[0]
    tm = min(m, 1024)
    tn = 512
    gate_blk = 2 * (D_MODEL // tn)
    mrows = mod.shape[1]
    return pl.pallas_call(
        functools.partial(_out_proj_body, per_row=per_row, tm=tm),
        out_shape=jax.ShapeDtypeStruct((m, D_MODEL), F32),
        grid=(m // tm, D_MODEL // tn),
        in_specs=[pl.BlockSpec((tm, RET_HV), lambda i, n: (i, 0)),
                  pl.BlockSpec((1, RET_HV, tn), lambda i, n: (j, 0, n)),
                  pl.BlockSpec((tm, tn), lambda i, n: (row_tile0 + i, n)),
                  pl.BlockSpec((1, mrows, tn), lambda i, n: (layer, 0, gate_blk + n))],
        out_specs=pl.BlockSpec((tm, tn), lambda i, n: (i, n)),
        compiler_params=_cparams(("arbitrary", "arbitrary")),
        name="ret_out_proj",
    )(u, w_out, x_all, mod)


def _ffn_pre_body(xp_ref, xs_ref, shp, scp, shs, scs, g_ref, wr_ref, br_ref,
                  hw_ref, idx_ref, gate_ref, rank_ref, cnt_ref, carry):
    i = pl.program_id(0)
    tm = TOK_TILE

    @pl.when(i == 0)
    def _():
        carry[...] = jnp.zeros_like(carry)

    x = jnp.where(i == N_PROMPT_TILES, xs_ref[...], xp_ref[...])
    h = _norm_mod(x, g_ref[...], _pick_mod(i, shp, shs), _pick_mod(i, scp, scs))

    bits = lax.bitcast_convert_type(h.astype(BF16).astype(F32), U32)
    half = D_MODEL // 2
    words = (bits[:, :half] >> 16) | (bits[:, half:] & jnp.uint32(0xFFFF0000))
    for a in range(PACK_ROWS):
        hw_ref[pl.ds(a, tm, stride=PACK_ROWS), :] = words[:, a * LANES:(a + 1) * LANES]

    lane = lax.broadcasted_iota(I32, (tm, LANES), 1)
    lane_f = lane.astype(F32)
    logits = jnp.dot(h, wr_ref[0], preferred_element_type=F32, precision=lax.Precision.HIGHEST) + br_ref[0]
    logits = jnp.where(lane < N_EXPERTS, logits, -jnp.inf)

    hots, vals = [], []
    idx_out = jnp.zeros((tm, LANES), F32)
    for kk in range(TOP_K):
        m = jnp.max(logits, axis=1, keepdims=True)
        pick = jnp.min(jnp.where(logits == m, lane_f, float(LANES)), axis=1, keepdims=True)
        hot = lane_f == pick
        logits = jnp.where(hot, -jnp.inf, logits)
        hots.append(hot)
        vals.append(m)
        idx_out = jnp.where(lane == kk, pick, idx_out)
    exps = [jnp.exp(v - vals[0]) for v in vals]
    den = exps[0] + exps[1] + exps[2] + exps[3]
    gate_out = jnp.zeros((tm, LANES), F32)
    for kk in range(TOP_K):
        gate_out = jnp.where(lane == kk, exps[kk] / den, gate_out)

    cnt = jnp.zeros((tm, LANES), F32)
    for hot in hots:
        cnt = cnt + hot.astype(F32)
    r_i = lax.broadcasted_iota(I32, (tm, tm), 0)
    c_i = lax.broadcasted_iota(I32, (tm, tm), 1)
    lower = jnp.where(c_i < r_i, 1.0, 0.0).astype(BF16)
    base = _bdot(lower, cnt.astype(BF16)) + carry[0:1, :]
    rank_out = jnp.zeros((tm, LANES), F32)
    for kk in range(TOP_K):
        rk = jnp.sum(jnp.where(hots[kk], base, 0.0), axis=1, keepdims=True)
        rank_out = jnp.where(lane == kk, rk, rank_out)
    total = carry[0:1, :] + jnp.sum(cnt, axis=0, keepdims=True)
    carry[0:1, :] = total

    idx_ref[...] = idx_out.astype(I32)
    gate_ref[...] = gate_out
    rank_ref[...] = rank_out.astype(I32)

    @pl.when(i == pl.num_programs(0) - 1)
    def _():
        cnt_ref[...] = jnp.broadcast_to(total, cnt_ref.shape).astype(I32)


def _ffn_pre(x_p, x_s, mod_p, mod_s, layer, g, w_router_pad, b_router_pad):
    tm = TOK_TILE
    tok = lambda i: (i, 0)
    return pl.pallas_call(
        _ffn_pre_body,
        out_shape=(jax.ShapeDtypeStruct((T_ALL * PACK_ROWS, LANES), U32),
                   jax.ShapeDtypeStruct((T_ALL, LANES), I32),
                   jax.ShapeDtypeStruct((T_ALL, LANES), F32),
                   jax.ShapeDtypeStruct((T_ALL, LANES), I32),
                   jax.ShapeDtypeStruct((8, LANES), I32)),
        grid=(N_TOK_TILES,),
        in_specs=[pl.BlockSpec((tm, D_MODEL), lambda i: (jnp.minimum(i, N_PROMPT_TILES - 1), 0)),
                  pl.BlockSpec((tm, D_MODEL), lambda i: (0, 0)),
                  pl.BlockSpec((1, 8, D_MODEL), lambda i: (layer, 0, 3)),
                  pl.BlockSpec((1, 8, D_MODEL), lambda i: (layer, 0, 4)),
                  pl.BlockSpec((1, DEC_BATCH, D_MODEL), lambda i: (layer, 0, 3)),
                  pl.BlockSpec((1, DEC_BATCH, D_MODEL), lambda i: (layer, 0, 4)),
                  pl.BlockSpec((1, D_MODEL), lambda i: (0, 0)),
                  pl.BlockSpec((1, D_MODEL, LANES), lambda i: (layer, 0, 0)),
                  pl.BlockSpec((1, 1, LANES), lambda i: (layer, 0, 0))],
        out_specs=(pl.BlockSpec((tm * PACK_ROWS, LANES), tok),
                   pl.BlockSpec((tm, LANES), tok),
                   pl.BlockSpec((tm, LANES), tok),
                   pl.BlockSpec((tm, LANES), tok),
                   pl.BlockSpec((8, LANES), lambda i: (0, 0))),
        scratch_shapes=[pltpu.VMEM((8, LANES), F32)],
        compiler_params=_cparams(("arbitrary",)),
        name="ffn_pre",
    )(x_p, x_s, mod_p, mod_p, mod_s, mod_s, g, w_router_pad, b_router_pad)


def _route(idx, rank, counts):
    e = idx[:, :TOP_K]
    r = rank[:, :TOP_K]
    cnt = counts[0, :N_EXPERTS]
    ntile = (cnt + MOE_SUB - 1) // MOE_SUB
    tile_end = jnp.cumsum(ntile)
    tile_start = tile_end - ntile
    pos = (tile_start[e] * MOE_SUB + r).astype(I32)
    tok = jnp.broadcast_to(jnp.arange(T_ALL, dtype=I32)[:, None], pos.shape)
    row_tok = jnp.zeros((MOE_ROWS,), I32).at[pos.reshape(-1)].set(tok.reshape(-1))
    npass = (ntile + MOE_NSUB - 1) // MOE_NSUB
    item_end = jnp.cumsum(npass)
    item_start = item_end - npass
    n_items = item_end[-1]
    it = jnp.arange(MOE_ITEMS + 1, dtype=I32)
    it_c = jnp.minimum(it, n_items - 1)
    item_e = jnp.minimum(jnp.searchsorted(item_end, it_c, side="right"), N_EXPERTS - 1).astype(I32)
    p = it_c - item_start[item_e]
    item_row0 = ((tile_start[item_e] + p * MOE_NSUB) * MOE_SUB).astype(I32)
    item_nsub = jnp.where(it < n_items, jnp.minimum(ntile[item_e] - p * MOE_NSUB, MOE_NSUB), 0).astype(I32)
    return pos.reshape(-1), row_tok, item_e, item_row0, item_nsub, n_items.reshape(1).astype(I32)


def _prefetch_cover(n_cur, n_nxt):
    n_both = jnp.minimum(n_cur, n_nxt)
    npair = jnp.maximum(n_cur - 1, 0) // 2
    pairs_cov = jnp.clip((n_both - 1) // 2, 0, npair)
    has_single = jnp.logical_and(n_cur >= 2, (n_cur - 1) % 2 == 1)
    single_cov = jnp.where(jnp.logical_and(has_single, n_both == n_cur), 1, 0)
    total = jnp.where(n_both >= 1, 1 + 2 * pairs_cov + single_cov, 0)
    return pairs_cov, single_cov, total


def _experts_body(item_e, item_row0, item_nsub, n_items, row_tok,
                  hw_hbm, wg_ref, wu_ref, bg_ref, bu_ref, wd_ref, bd_ref,
                  ys_hbm,
                  xraw, hbuf, wgb, wub, wdb, ybuf, gsem, ysem, pend, pend_row, pend_col):
    it = pl.program_id(0)
    j = pl.program_id(1)
    nsub = item_nsub[it]
    row0 = item_row0[it]
    nsub_next = item_nsub[it + 1]
    row0_next = item_row0[it + 1]
    n_pref = jnp.where(it > 0, _prefetch_cover(item_nsub[jnp.maximum(it - 1, 0)], nsub)[2], 0)
    npair = jnp.maximum(nsub - 1, 0) // 2
    has_single = jnp.logical_and(nsub >= 2, (nsub - 1) % 2 == 1)
    active = it < n_items[0]
    ts = MOE_SUB
    tn = MOE_TN
    pf_rows = ts // MOE_JD

    def chunk(c):
        return slice(c * tn, (c + 1) * tn)

    def row_copy(base, r, s):
        src = hw_hbm.at[pl.ds(pl.multiple_of(row_tok[base + r] * PACK_ROWS, PACK_ROWS), PACK_ROWS)]
        dst = xraw.at[pl.ds(pl.multiple_of(r * PACK_ROWS, PACK_ROWS), PACK_ROWS)]
        return pltpu.make_async_copy(src, dst, gsem.at[s])

    def out_copy(row, slot, col, m):
        dst = ys_hbm.at[pl.ds(pl.multiple_of(row, ts), m * ts), pl.ds(pl.multiple_of(col * tn, tn), tn)]
        return pltpu.make_async_copy(ybuf.at[slot, pl.ds(0, m * ts)], dst, ysem.at[slot])

    def out_wait(slot):
        for m in (1, 2):
            @pl.when(pend[slot] == m)
            def _():
                out_copy(pend_row[slot], slot, pend_col[slot], m).wait()
                pend[slot] = 0

    def out_start(row, slot, col, m):
        out_copy(row, slot, col, m).start()
        pend[slot] = m
        pend_row[slot] = row
        pend_col[slot] = col

    @pl.when(jnp.logical_and(it == 0, j == 0))
    def _():
        pend[0] = 0
        pend[1] = 0

    def swiglu(gt, up):
        gt = jnp.minimum(gt, SWIGLU_LIMIT)
        up = jnp.clip(up, -SWIGLU_LIMIT, SWIGLU_LIMIT)
        return ((up + 1.0) * gt * jax.nn.sigmoid(SWIGLU_ALPHA * gt)).astype(BF16)

    @pl.when(jnp.logical_and(active, j == 0))
    def _gather():
        def issue(r, c):
            row_copy(row0, r, lax.shift_right_logical(r, int(np.log2(ts)))).start()
            return c
        lax.fori_loop(n_pref * ts, nsub * ts, issue, 0)

        def land(s, c):
            rows = pl.ds(pl.multiple_of(s * ts * PACK_ROWS, ts * PACK_ROWS), ts * PACK_ROWS)
            pltpu.make_async_copy(hw_hbm.at[pl.ds(0, ts * PACK_ROWS)], xraw.at[rows], gsem.at[s]).wait()
            return c
        lax.fori_loop(0, nsub, land, 0)

    def unpack(w, high):
        bits = (w & jnp.uint32(0xFFFF0000)) if high else (w << 16)
        return lax.bitcast_convert_type(bits, F32).astype(BF16)

    def x_cols(r0, lane_blocks, high):
        parts = [unpack(xraw[pl.ds(r0 * PACK_ROWS + a, ts, stride=PACK_ROWS), :], high) for a in lane_blocks]
        return jnp.concatenate(parts, axis=1)

    @pl.when(jnp.logical_and(active, j < MOE_JGU))
    def _gate_up():
        per_half = (D_MODEL // 2) // tn
        gt = jnp.broadcast_to(bg_ref[0, 0], (ts, tn))
        up = jnp.broadcast_to(bu_ref[0, 0], (ts, tn))
        for c in range(D_MODEL // tn):
            wg_c = wg_ref[0, 0, chunk(c), :].astype(BF16)
            wu_c = wu_ref[0, 0, chunk(c), :].astype(BF16)
            wgb[chunk(c), :] = wg_c
            wub[chunk(c), :] = wu_c
            blocks = range((c % per_half) * (tn // LANES), (c % per_half + 1) * (tn // LANES))
            x_c = x_cols(0, blocks, c >= per_half)
            gt = gt + _bdot(x_c, wg_c)
            up = up + _bdot(x_c, wu_c)
        hbuf[j, 0:ts, :] = swiglu(gt, up)

        def gu_tile(s, m):
            r0 = pl.multiple_of(s * ts, ts)
            ws = [xraw[pl.ds(r0 * PACK_ROWS + a, m * ts, stride=PACK_ROWS), :] for a in range(PACK_ROWS)]
            x = jnp.concatenate([unpack(w, False) for w in ws] + [unpack(w, True) for w in ws], axis=1)
            hbuf[j, pl.ds(r0, m * ts), :] = swiglu(_bdot(x, wgb[...]) + bg_ref[0, 0],
                                                   _bdot(x, wub[...]) + bu_ref[0, 0])

        def pair(p, c):
            gu_tile(1 + 2 * p, 2)
            return c
        lax.fori_loop(0, npair, pair, 0)

        @pl.when(has_single)
        def _():
            gu_tile(nsub - 1, 1)

    @pl.when(jnp.logical_and(active, j >= MOE_JGU))
    def _down():
        jd = j - MOE_JGU
        pairs_cov, single_cov, _ = _prefetch_cover(nsub, nsub_next)

        def down_tile(s, m, slot, first, prefetch):
            rows = slice(0, ts) if first else pl.ds(pl.multiple_of(s * ts, ts), m * ts)
            out_wait(slot)
            acc = jnp.broadcast_to(bd_ref[0, 0], (m * ts, tn))
            for c in range(MOE_JGU):
                if first:
                    wd_c = wd_ref[0, 0, chunk(c), :].astype(BF16)
                    wdb[chunk(c), :] = wd_c
                else:
                    wd_c = wdb[chunk(c), :]
                acc = acc + _bdot(hbuf[c, rows, :], wd_c)
            ybuf[slot, 0:m * ts, :] = acc
            out_start(row0 + s * ts, slot, jd, m)
            if prefetch:
                for mm in range(m):
                    for q in range(pf_rows):
                        row_copy(row0_next, (s + mm) * ts + jd * pf_rows + q, s + mm).start()

        @pl.when(nsub_next >= 1)
        def _():
            down_tile(0, 1, 0, True, True)

        @pl.when(nsub_next < 1)
        def _():
            down_tile(0, 1, 0, True, False)

        def pair_pf(p, c):
            down_tile(1 + 2 * p, 2, (1 + p) % 2, False, True)
            return c
        lax.fori_loop(0, pairs_cov, pair_pf, 0)

        def pair(p, c):
            down_tile(1 + 2 * p, 2, (1 + p) % 2, False, False)
            return c
        lax.fori_loop(pairs_cov, npair, pair, 0)

        @pl.when(jnp.logical_and(has_single, single_cov == 1))
        def _():
            down_tile(nsub - 1, 1, (1 + npair) % 2, False, True)

        @pl.when(jnp.logical_and(has_single, single_cov == 0))
        def _():
            down_tile(nsub - 1, 1, (1 + npair) % 2, False, False)

    @pl.when(jnp.logical_and(it == pl.num_programs(0) - 1, j == pl.num_programs(1) - 1))
    def _fill_tail():
        out_wait(0)
        out_wait(1)
        ybuf[0, 0:ts, :] = jnp.zeros((ts, tn), F32)
        last = n_items[0] - 1
        used = item_row0[last] // ts + item_nsub[last]

        def tail_copy(t, c):
            dst = ys_hbm.at[pl.ds(pl.multiple_of(t * ts, ts), ts), pl.ds(c * tn, tn)]
            return pltpu.make_async_copy(ybuf.at[0, pl.ds(0, ts)], dst, ysem.at[0])

        def fill(t, carry):
            for c in range(MOE_JD):
                tail_copy(t, c).start()
            for c in range(MOE_JD):
                tail_copy(t, c).wait()
            return carry
        lax.fori_loop(used, MOE_ROW_TILES, fill, 0)


def _experts(route, hw, w_gate_up, b_gate_up, w_down, b_down, layer):
    _, row_tok, item_e, item_row0, item_nsub, n_items = route
    tn = MOE_TN
    jgu = MOE_JGU
    gu_c = lambda j: jnp.minimum(j, jgu - 1)
    dn_c = lambda j: jnp.maximum(j - jgu, 0)
    grid_spec = pltpu.PrefetchScalarGridSpec(
        num_scalar_prefetch=5,
        grid=(MOE_ITEMS, MOE_JGU + MOE_JD),
        in_specs=[
            pl.BlockSpec(memory_space=pl.ANY),
            pl.BlockSpec((1, 1, D_MODEL, tn), lambda it, j, ie, *_: (layer, ie[it], 0, gu_c(j))),
            pl.BlockSpec((1, 1, D_MODEL, tn), lambda it, j, ie, *_: (layer, ie[it], 0, jgu + gu_c(j))),
            pl.BlockSpec((1, 1, 1, tn), lambda it, j, ie, *_: (layer, ie[it], 0, gu_c(j))),
            pl.BlockSpec((1, 1, 1, tn), lambda it, j, ie, *_: (layer, ie[it], 0, jgu + gu_c(j))),
            pl.BlockSpec((1, 1, D_FF, tn), lambda it, j, ie, *_: (layer, ie[it], 0, dn_c(j))),
            pl.BlockSpec((1, 1, 1, tn), lambda it, j, ie, *_: (layer, ie[it], 0, dn_c(j))),
        ],
        out_specs=pl.BlockSpec(memory_space=pl.ANY),
        scratch_shapes=[
            pltpu.VMEM((MOE_NSUB * MOE_SUB * PACK_ROWS, LANES), U32),
            pltpu.VMEM((MOE_JGU, MOE_NSUB * MOE_SUB, tn), BF16),
            pltpu.VMEM((D_MODEL, tn), BF16),
            pltpu.VMEM((D_MODEL, tn), BF16),
            pltpu.VMEM((D_FF, tn), BF16),
            pltpu.VMEM((2, 2 * MOE_SUB, tn), F32),
            pltpu.SemaphoreType.DMA((MOE_NSUB,)),
            pltpu.SemaphoreType.DMA((2,)),
            pltpu.SMEM((2,), I32),
            pltpu.SMEM((2,), I32),
            pltpu.SMEM((2,), I32),
        ],
    )
    ne = w_gate_up.shape[1]
    return pl.pallas_call(
        _experts_body,
        out_shape=jax.ShapeDtypeStruct((MOE_ROWS, D_MODEL), F32),
        grid_spec=grid_spec,
        compiler_params=_cparams(("arbitrary", "arbitrary")),
        name="moe_experts",
    )(item_e, item_row0, item_nsub, n_items, row_tok,
      hw, w_gate_up, w_gate_up,
      b_gate_up.reshape(DEPTH, ne, 1, 2 * D_FF), b_gate_up.reshape(DEPTH, ne, 1, 2 * D_FF),
      w_down, b_down.reshape(DEPTH, ne, 1, D_MODEL))


def _combine_body(pos, ys_hbm, xp_ref, xs_ref, gate_ref, gtp, gts, o_ref, ybuf, sem):
    i = pl.program_id(0)
    tm = TOK_TILE

    def issue_tile(tile, slot):
        base = tile * (tm * TOP_K)

        def issue(t, c):
            for kk in range(TOP_K):
                pltpu.make_async_copy(ys_hbm.at[pos[base + t * TOP_K + kk]], ybuf.at[slot, kk, t],
                                      sem.at[slot]).start()
            return c
        lax.fori_loop(0, tm, issue, 0)

    @pl.when(i == 0)
    def _():
        issue_tile(0, 0)

    @pl.when(i + 1 < pl.num_programs(0))
    def _():
        issue_tile(i + 1, (i + 1) % 2)

    slot = i % 2
    for kk in range(TOP_K):
        pltpu.make_async_copy(ys_hbm.at[pl.ds(0, tm)], ybuf.at[slot, kk], sem.at[slot]).wait()

    gates = gate_ref[...]
    f = gates[:, 0:1] * ybuf[slot, 0]
    for kk in range(1, TOP_K):
        f = f + gates[:, kk:kk + 1] * ybuf[slot, kk]
    x = jnp.where(i == N_PROMPT_TILES, xs_ref[...], xp_ref[...])
    o_ref[...] = x + _pick_mod(i, gtp, gts) * f


def _combine(pos, ys, x_p, x_s, gates, mod_p, mod_s, layer):
    tm = TOK_TILE
    grid_spec = pltpu.PrefetchScalarGridSpec(
        num_scalar_prefetch=1,
        grid=(N_TOK_TILES,),
        in_specs=[pl.BlockSpec(memory_space=pl.ANY),
                  pl.BlockSpec((tm, D_MODEL), lambda i, *_: (jnp.minimum(i, N_PROMPT_TILES - 1), 0)),
                  pl.BlockSpec((tm, D_MODEL), lambda i, *_: (0, 0)),
                  pl.BlockSpec((tm, LANES), lambda i, *_: (i, 0)),
                  pl.BlockSpec((1, 8, D_MODEL), lambda i, *_: (layer, 0, 5)),
                  pl.BlockSpec((1, DEC_BATCH, D_MODEL), lambda i, *_: (layer, 0, 5))],
        out_specs=pl.BlockSpec((tm, D_MODEL), lambda i, *_: (i, 0)),
        scratch_shapes=[pltpu.VMEM((2, TOP_K, tm, D_MODEL), F32), pltpu.SemaphoreType.DMA((2,))],
    )
    return pl.pallas_call(
        _combine_body,
        out_shape=jax.ShapeDtypeStruct((T_ALL, D_MODEL), F32),
        grid_spec=grid_spec,
        compiler_params=_cparams(("arbitrary",)),
        name="moe_combine",
    )(pos, ys, x_p, x_s, gates, mod_p, mod_s)


def _final_norm_body(x_ref, g_ref, op_ref, os_ref):
    i = pl.program_id(0)
    x = x_ref[...]
    y = x * lax.rsqrt(jnp.mean(x * x, axis=-1, keepdims=True) + NORM_EPS) * g_ref[...]

    @pl.when(i < N_PROMPT_TILES)
    def _():
        op_ref[...] = y

    @pl.when(i == N_PROMPT_TILES)
    def _():
        os_ref[...] = y


def _final_norm(x_all, g):
    tm = TOK_TILE
    return pl.pallas_call(
        _final_norm_body,
        out_shape=(jax.ShapeDtypeStruct((TP, D_MODEL), F32),
                   jax.ShapeDtypeStruct((DEC_BATCH, D_MODEL), F32)),
        grid=(N_TOK_TILES,),
        in_specs=[pl.BlockSpec((tm, D_MODEL), lambda i: (i, 0)),
                  pl.BlockSpec((1, D_MODEL), lambda i: (0, 0))],
        out_specs=(pl.BlockSpec((tm, D_MODEL), lambda i: (jnp.minimum(i, N_PROMPT_TILES - 1), 0)),
                   pl.BlockSpec((tm, D_MODEL), lambda i: (0, 0))),
        compiler_params=_cparams(("arbitrary",)),
        name="final_norm",
    )(x_all, g)


def _rope_tables(pos):
    inv = ROPE_BASE ** (-jnp.linspace(0.0, 1.0, RET_DK // 2, dtype=F32))
    ang = pos.astype(F32)[:, None] * inv[None, :]
    return jnp.cos(ang), jnp.sin(ang)


def kernel(x_prompt, x_sample, state_pool, state_ret, c_prompt, c_sample, w_mod, b_mod, g_norm_mix, g_norm_ffn,
           pool_w, pool_scale, ret_w_in, ret_w_out, moe_w_router, moe_b_router, moe_w_gate_up, moe_b_gate_up,
           moe_w_down, moe_b_down, g_final):
    assert x_prompt.shape == (BATCH, SEQ, D_MODEL) and x_sample.shape == (DEC_BATCH, 1, D_MODEL)
    c_prompt8 = jnp.concatenate([c_prompt, jnp.zeros((8 - BATCH, D_MODEL), F32)], axis=0)
    mod_s, mod_p = _adaln(c_sample, c_prompt8, w_mod, b_mod)
    cos_p, sin_p = _rope_tables(jnp.arange(SEQ))
    cos_s, sin_s = _rope_tables(PAST_LEN + jnp.arange(1))
    w_router_pad = jnp.pad(moe_w_router, ((0, 0), (0, 0), (0, LANES - N_EXPERTS)))
    b_router_pad = jnp.pad(moe_b_router, ((0, 0), (0, LANES - N_EXPERTS))).reshape(DEPTH, 1, LANES)

    x_p_src, x_p_tile0 = x_prompt.reshape(TP, D_MODEL), 0
    x_s_src, x_s_rows0 = x_sample.reshape(DEC_BATCH, D_MODEL), 0
    x_all = None
    new_pool_p, new_pool_s, new_ret_p = [], [], []
    ret_s_out = None
    for layer in range(DEPTH):
        j = layer // N_MIXERS
        g_mix = g_norm_mix[layer].reshape(1, D_MODEL)
        g_ffn = g_norm_ffn[layer].reshape(1, D_MODEL)
        if layer % N_MIXERS == 0:
            scale = pool_scale[j].reshape(1, D_MODEL)
            x_p, pool_p = _pool_prompt(x_p_src, x_p_tile0, mod_p, layer, g_mix, pool_w, scale, j)
            x_s, pool_s = _pool_sample(x_s_src, x_s_rows0 // 16, state_pool, mod_s, layer, g_mix, pool_w, scale, j)
            new_pool_p.append(pool_p)
            new_pool_s.append(pool_s)
        else:
            h_all = _norm_mod_all(x_all, mod_p, mod_s, layer, g_mix)
            proj = _in_proj(h_all, ret_w_in, j)
            u_p, ret_p = _ret_prompt(proj, cos_p, sin_p)
            u_s, ret_s_out = _ret_sample(proj, cos_s, sin_s, state_ret, j, ret_s_out)
            new_ret_p.append(ret_p)
            x_p = _out_proj(u_p, ret_w_out, j, x_all, 0, mod_p, layer, per_row=False)
            x_s = _out_proj(u_s, ret_w_out, j, x_all, TP // DEC_BATCH, mod_s, layer, per_row=True)
        hw, idx, gates, rank, counts = _ffn_pre(x_p, x_s, mod_p, mod_s, layer, g_ffn, w_router_pad, b_router_pad)
        route = _route(idx, rank, counts)
        ys = _experts(route, hw, moe_w_gate_up, moe_b_gate_up, moe_w_down, moe_b_down, layer)
        x_all = _combine(route[0], ys, x_p, x_s, gates, mod_p, mod_s, layer)
        x_p_src, x_p_tile0 = x_all, 0
        x_s_src, x_s_rows0 = x_all, TP
    y_p, y_s = _final_norm(x_all, g_final.reshape(1, D_MODEL))
    return (y_p.reshape(BATCH, SEQ, D_MODEL), y_s.reshape(DEC_BATCH, 1, D_MODEL),
            jnp.stack(new_pool_p), jnp.stack(new_pool_s), jnp.stack(new_ret_p), ret_s_out)
```

```python
import functools

import jax
import jax.numpy as jnp
import numpy as np
from jax import lax
from jax.experimental import pallas as pl
from jax.experimental.pallas import tpu as pltpu

F32 = jnp.float32
BF16 = jnp.bfloat16
I32 = jnp.int32
U32 = jnp.uint32

D_MODEL = 2048
BATCH = 4
SEQ = 2048
DEPTH = 4
DEC_BATCH = 128
PAST_LEN = 16384
N_MIXERS = 2
POOL_WINDOWS = (2, 4, 8, 16)
POOL_GROUP_DIM = D_MODEL // len(POOL_WINDOWS)
POOL_BUF = max(POOL_WINDOWS) - 1
POOL_HALO = 16
RET_HEADS = 8
RET_DK = D_MODEL // RET_HEADS
RET_DV = 2 * D_MODEL // RET_HEADS
RET_CHUNK = 128
RET_HK = RET_HEADS * RET_DK
RET_HV = RET_HEADS * RET_DV
RET_IN_DIM = 2 * RET_HK + 2 * RET_HV
ROPE_BASE = 10000.0
N_EXPERTS = 32
TOP_K = 4
D_FF = D_MODEL
SWIGLU_LIMIT = 7.0
SWIGLU_ALPHA = 1.702
NORM_EPS = 1e-5

TP = BATCH * SEQ
T_ALL = TP + DEC_BATCH
TOK_TILE = 128
N_TOK_TILES = T_ALL // TOK_TILE
N_PROMPT_TILES = TP // TOK_TILE
TILES_PER_SEQ = SEQ // TOK_TILE
LANES = 128
PACK_ROWS = D_MODEL // 2 // LANES

MOE_SUB = 256
MOE_NSUB = 8
MOE_TN = 512
MOE_WSPLIT = 4
MOE_JGU = D_FF // MOE_TN
MOE_JD = D_MODEL // MOE_TN
N_ASSIGN = T_ALL * TOP_K
MOE_ROW_TILES = -(-(N_ASSIGN + N_EXPERTS * (MOE_SUB - 1)) // MOE_SUB)
MOE_ROWS = MOE_ROW_TILES * MOE_SUB
MOE_ITEMS = N_EXPERTS + MOE_ROW_TILES // MOE_NSUB

VMEM_LIMIT = 56 * 1024 * 1024


def _cparams(sem, vmem=VMEM_LIMIT):
    return pltpu.CompilerParams(dimension_semantics=sem, vmem_limit_bytes=vmem)


def _norm_mod(x, g, shift, scale):
    ms = jnp.mean(x * x, axis=-1, keepdims=True)
    y = x * lax.rsqrt(ms + NORM_EPS)
    return (y * g) * (1.0 + scale) + shift


def _silu(x):
    return x * jax.nn.sigmoid(x)


def _bdot(a, b):
    return jnp.dot(a, b, preferred_element_type=F32)


def _adaln_body(cs_ref, cp_ref, w_ref, b_ref, os_ref, op_ref):
    w = w_ref[0].astype(BF16)
    b = b_ref[0]
    os_ref[0] = _bdot(_silu(cs_ref[...]).astype(BF16), w) + b
    op_ref[0] = _bdot(_silu(cp_ref[...]).astype(BF16), w) + b


def _adaln(c_sample, c_prompt8, w_mod, b_mod):
    nl, d, n6 = w_mod.shape
    tn = 1024
    return pl.pallas_call(
        _adaln_body,
        out_shape=(jax.ShapeDtypeStruct((nl, DEC_BATCH, n6), F32),
                   jax.ShapeDtypeStruct((nl, 8, n6), F32)),
        grid=(nl, n6 // tn),
        in_specs=[pl.BlockSpec((DEC_BATCH, d), lambda l, n: (0, 0)),
                  pl.BlockSpec((8, d), lambda l, n: (0, 0)),
                  pl.BlockSpec((1, d, tn), lambda l, n: (l, 0, n)),
                  pl.BlockSpec((1, 1, tn), lambda l, n: (l, 0, n))],
        out_specs=(pl.BlockSpec((1, DEC_BATCH, tn), lambda l, n: (l, 0, n)),
                   pl.BlockSpec((1, 8, tn), lambda l, n: (l, 0, n))),
        compiler_params=_cparams(("arbitrary", "arbitrary")),
        name="adaln",
    )(c_sample, c_prompt8, w_mod, b_mod.reshape(nl, 1, n6))


def _pick_mod(i, mp_ref, ms_ref):
    b = jnp.minimum(i // TILES_PER_SEQ, BATCH - 1)
    return jnp.where(i == N_PROMPT_TILES, ms_ref[0], mp_ref[0, pl.ds(b, 1), :])


def _pool_group_dots(diff, pw_ref):
    ys = []
    for gi in range(len(POOL_WINDOWS)):
        cols = slice(gi * POOL_GROUP_DIM, (gi + 1) * POOL_GROUP_DIM)
        ys.append(_bdot(diff[gi] if isinstance(diff, (list, tuple)) else diff[:, cols],
                        pw_ref[0, gi].astype(BF16)))
    return jnp.concatenate(ys, axis=1)


def _pool_prompt_body(x_ref, sh_ref, sc_ref, gt_ref, g_ref, pw_ref, ps_ref, xo_ref, np_ref, hbuf, *, ts):
    b = pl.program_id(0)
    s = pl.program_id(1)

    @pl.when(s == 0)
    def _():
        hbuf[0:POOL_HALO, :] = jnp.zeros((POOL_HALO, D_MODEL), F32)

    x = x_ref[...]
    shift = sh_ref[0, pl.ds(b, 1), :]
    scale = sc_ref[0, pl.ds(b, 1), :]
    gate = gt_ref[0, pl.ds(b, 1), :]
    h = _norm_mod(x, g_ref[...], shift, scale)
    hbuf[POOL_HALO:, :] = h
    t = s * ts + lax.broadcasted_iota(I32, (ts, 1), 0)
    diffs = []
    for gi, w in enumerate(POOL_WINDOWS):
        cols = slice(gi * POOL_GROUP_DIM, (gi + 1) * POOL_GROUP_DIM)
        acc = hbuf[:, cols]
        k = 1
        while k < w:
            acc = acc + pltpu.roll(acc, k, 0)
            k *= 2
        cnt = jnp.minimum(t + 1, w).astype(F32)
        diffs.append((acc[POOL_HALO:, :] / cnt - h[:, cols]).astype(BF16))
    y = _pool_group_dots(diffs, pw_ref) * ps_ref[...]
    xo_ref[...] = x + gate * y

    @pl.when(s == pl.num_programs(1) - 1)
    def _():
        np_ref[0] = hbuf[pl.ds(POOL_HALO + ts - POOL_BUF, POOL_BUF), :]

    hbuf[0:POOL_HALO, :] = hbuf[ts:ts + POOL_HALO, :]


def _pool_prompt(x2d, row_tile0, mod_p, layer, g, pool_w, pool_scale, j):
    ts = 256
    ns = SEQ // ts
    mspec = lambda k: pl.BlockSpec((1, 8, D_MODEL), lambda b, s: (layer, 0, k))
    return pl.pallas_call(
        functools.partial(_pool_prompt_body, ts=ts),
        out_shape=(jax.ShapeDtypeStruct((TP, D_MODEL), F32),
                   jax.ShapeDtypeStruct((BATCH, POOL_BUF, D_MODEL), F32)),
        grid=(BATCH, ns),
        in_specs=[pl.BlockSpec((ts, D_MODEL), lambda b, s: (row_tile0 + b * ns + s, 0)),
                  mspec(0), mspec(1), mspec(2),
                  pl.BlockSpec((1, D_MODEL), lambda b, s: (0, 0)),
                  pl.BlockSpec((1, len(POOL_WINDOWS), POOL_GROUP_DIM, POOL_GROUP_DIM),
                               lambda b, s: (j, 0, 0, 0)),
                  pl.BlockSpec((1, D_MODEL), lambda b, s: (0, 0))],
        out_specs=(pl.BlockSpec((ts, D_MODEL), lambda b, s: (b * ns + s, 0)),
                   pl.BlockSpec((1, POOL_BUF, D_MODEL), lambda b, s: (b, 0, 0))),
        scratch_shapes=[pltpu.VMEM((POOL_HALO + ts, D_MODEL), F32)],
        compiler_params=_cparams(("arbitrary", "arbitrary")),
        name="pool_prompt",
    )(x2d, mod_p, mod_p, mod_p, g, pool_w, pool_scale)


def _pool_sample_body(x_ref, st_ref, sh_ref, sc_ref, gt_ref, g_ref, pw_ref, ps_ref, xo_ref, npo_ref, dbuf, *, sb):
    x = x_ref[...]
    h = _norm_mod(x, g_ref[...], sh_ref[0], sc_ref[0])
    row = lax.broadcasted_iota(I32, (POOL_BUF, 1), 0)
    for s in range(sb):
        st = st_ref[0, s]
        hs = h[s:s + 1, :]
        parts = []
        for gi, w in enumerate(POOL_WINDOWS):
            cols = slice(gi * POOL_GROUP_DIM, (gi + 1) * POOL_GROUP_DIM)
            tail = jnp.sum(jnp.where(row >= POOL_BUF - (w - 1), st[:, cols], 0.0), axis=0, keepdims=True)
            parts.append((tail + hs[:, cols]) / float(w) - hs[:, cols])
        dbuf[s:s + 1, :] = jnp.concatenate(parts, axis=1)
        npo_ref[s, 0:POOL_BUF - 1, :] = st[1:POOL_BUF, :]
        npo_ref[s, POOL_BUF - 1:POOL_BUF, :] = hs
    y = _pool_group_dots(dbuf[...].astype(BF16), pw_ref) * ps_ref[...]
    xo_ref[...] = x + gt_ref[0] * y


def _pool_sample(x2d, row_tile0, state_pool, mod_s, layer, g, pool_w, pool_scale, j):
    sb = 16
    mspec = lambda k: pl.BlockSpec((1, sb, D_MODEL), lambda i: (layer, i, k))
    return pl.pallas_call(
        functools.partial(_pool_sample_body, sb=sb),
        out_shape=(jax.ShapeDtypeStruct((DEC_BATCH, D_MODEL), F32),
                   jax.ShapeDtypeStruct((DEC_BATCH, POOL_BUF, D_MODEL), F32)),
        grid=(DEC_BATCH // sb,),
        in_specs=[pl.BlockSpec((sb, D_MODEL), lambda i: (row_tile0 + i, 0)),
                  pl.BlockSpec((1, sb, POOL_BUF, D_MODEL), lambda i: (j, i, 0, 0)),
                  mspec(0), mspec(1), mspec(2),
                  pl.BlockSpec((1, D_MODEL), lambda i: (0, 0)),
                  pl.BlockSpec((1, len(POOL_WINDOWS), POOL_GROUP_DIM, POOL_GROUP_DIM),
                               lambda i: (j, 0, 0, 0)),
                  pl.BlockSpec((1, D_MODEL), lambda i: (0, 0))],
        out_specs=(pl.BlockSpec((sb, D_MODEL), lambda i: (i, 0)),
                   pl.BlockSpec((sb, POOL_BUF, D_MODEL), lambda i: (i, 0, 0))),
        scratch_shapes=[pltpu.VMEM((sb, D_MODEL), F32)],
        compiler_params=_cparams(("arbitrary",)),
        name="pool_sample",
    )(x2d, state_pool, mod_s, mod_s, mod_s, g, pool_w, pool_scale)


def _norm_mod_body(x_ref, shp, scp, shs, scs, g_ref, o_ref):
    i = pl.program_id(0)
    h = _norm_mod(x_ref[...], g_ref[...], _pick_mod(i, shp, shs), _pick_mod(i, scp, scs))
    o_ref[...] = h.astype(BF16)


def _norm_mod_all(x_all, mod_p, mod_s, layer, g):
    return pl.pallas_call(
        _norm_mod_body,
        out_shape=jax.ShapeDtypeStruct((T_ALL, D_MODEL), BF16),
        grid=(N_TOK_TILES,),
        in_specs=[pl.BlockSpec((TOK_TILE, D_MODEL), lambda i: (i, 0)),
                  pl.BlockSpec((1, 8, D_MODEL), lambda i: (layer, 0, 0)),
                  pl.BlockSpec((1, 8, D_MODEL), lambda i: (layer, 0, 1)),
                  pl.BlockSpec((1, DEC_BATCH, D_MODEL), lambda i: (layer, 0, 0)),
                  pl.BlockSpec((1, DEC_BATCH, D_MODEL), lambda i: (layer, 0, 1)),
                  pl.BlockSpec((1, D_MODEL), lambda i: (0, 0))],
        out_specs=pl.BlockSpec((TOK_TILE, D_MODEL), lambda i: (i, 0)),
        compiler_params=_cparams(("arbitrary",)),
        name="ret_norm_mod",
    )(x_all, mod_p, mod_p, mod_s, mod_s, g)


def _in_proj_body(a_ref, w_ref, o_ref):
    o_ref[...] = _bdot(a_ref[...], w_ref[0].astype(BF16))


def _in_proj(h_all, w_in, j):
    tm = T_ALL // 4
    tn = 512
    return pl.pallas_call(
        _in_proj_body,
        out_shape=jax.ShapeDtypeStruct((T_ALL, RET_IN_DIM), F32),
        grid=(T_ALL // tm, RET_IN_DIM // tn),
        in_specs=[pl.BlockSpec((tm, D_MODEL), lambda m, n: (m, 0)),
                  pl.BlockSpec((1, D_MODEL, tn), lambda m, n: (j, 0, n))],
        out_specs=pl.BlockSpec((tm, tn), lambda m, n: (m, n)),
        compiler_params=_cparams(("arbitrary", "arbitrary")),
        name="ret_in_proj",
    )(h_all, w_in)


def _rope(x, cos, sin, scale=None):
    half = RET_DK // 2
    x1 = x[:, :half]
    x2 = x[:, half:]
    out = jnp.concatenate([x1 * cos - x2 * sin, x1 * sin + x2 * cos], axis=1)
    return out if scale is None else out * scale


def _log_decay(h, shape):
    hf = jnp.full(shape, h, I32).astype(F32)
    return jnp.log(1.0 - jnp.exp2(-5.0 - hf))


def _group_norm_gate(o, g):
    o = o * lax.rsqrt(jnp.mean(o * o, axis=-1, keepdims=True) + NORM_EPS)
    return _silu(g) * o


def _ret_prompt_body(q_ref, k_ref, v_ref, g_ref, cos_ref, sin_ref, u_ref, s_ref):
    c = pl.program_id(1)
    cn = RET_CHUNK

    @pl.when(c == 0)
    def _():
        s_ref[...] = jnp.zeros_like(s_ref)

    cos = cos_ref[...]
    sin = sin_ref[...]
    i_col = lax.broadcasted_iota(I32, (cn, 1), 0).astype(F32)
    j_row = lax.broadcasted_iota(I32, (1, cn), 1).astype(F32)
    rel = i_col - j_row
    for h in range(RET_HEADS):
        kcols = slice(h * RET_DK, (h + 1) * RET_DK)
        vcols = slice(h * RET_DV, (h + 1) * RET_DV)
        q = _rope(q_ref[:, kcols], cos, sin)
        k = _rope(k_ref[:, kcols], cos, sin, RET_DK ** -0.5)
        v = v_ref[:, vcols].astype(BF16)
        intra = jnp.where(rel >= 0, jnp.exp(_log_decay(h, (cn, cn)) * jnp.maximum(rel, 0.0)), 0.0)
        lg_col = _log_decay(h, (cn, 1))
        q_dec = jnp.exp(lg_col * (i_col + 1.0))
        k_dec = jnp.exp(lg_col * (cn - 1.0 - i_col))
        c_dec = jnp.exp(_log_decay(h, (1, 1)) * float(cn))

        state = s_ref[0, h]
        s = lax.dot_general(q.astype(BF16), k.astype(BF16), (((1,), (1,)), ((), ())),
                            preferred_element_type=F32) * intra
        o = _bdot(s.astype(BF16), v) + _bdot((q * q_dec).astype(BF16), state.astype(BF16))
        kd_t = (k * k_dec).T.astype(BF16)
        s_ref[0, h] = c_dec * state + _bdot(kd_t, v)
        u_ref[:, vcols] = _group_norm_gate(o, g_ref[:, vcols]).astype(BF16)


def _ret_prompt(proj, cos, sin):
    nc = SEQ // RET_CHUNK
    cn = RET_CHUNK
    row = lambda b, c: b * nc + c
    return pl.pallas_call(
        _ret_prompt_body,
        out_shape=(jax.ShapeDtypeStruct((TP, RET_HV), BF16),
                   jax.ShapeDtypeStruct((BATCH, RET_HEADS, RET_DK, RET_DV), F32)),
        grid=(BATCH, nc),
        in_specs=[pl.BlockSpec((cn, RET_HK), lambda b, c: (row(b, c), 0)),
                  pl.BlockSpec((cn, RET_HK), lambda b, c: (row(b, c), 1)),
                  pl.BlockSpec((cn, RET_HV), lambda b, c: (row(b, c), 1)),
                  pl.BlockSpec((cn, RET_HV), lambda b, c: (row(b, c), 2)),
                  pl.BlockSpec((cn, RET_DK // 2), lambda b, c: (c, 0)),
                  pl.BlockSpec((cn, RET_DK // 2), lambda b, c: (c, 0))],
        out_specs=(pl.BlockSpec((cn, RET_HV), lambda b, c: (row(b, c), 0)),
                   pl.BlockSpec((1, RET_HEADS, RET_DK, RET_DV), lambda b, c: (b, 0, 0, 0))),
        compiler_params=_cparams(("arbitrary", "arbitrary")),
        name="ret_prompt",
    )(proj, proj, proj, proj, cos, sin)


def _ret_sample_body(q_ref, k_ref, v_ref, g_ref, cos_ref, sin_ref, si_ref, *rest, sb, aliased):
    if aliased:
        rest = rest[1:]
    u_ref, so_ref, obuf = rest
    h = pl.program_id(1)
    cos = cos_ref[...]
    sin = sin_ref[...]
    gamma = jnp.exp(_log_decay(h, (1, 1)))
    q = _rope(q_ref[...], cos, sin)
    k = _rope(k_ref[...], cos, sin, RET_DK ** -0.5)
    v = v_ref[...]
    qk = jnp.sum(q * k, axis=1, keepdims=True)
    pad = jnp.zeros((LANES - sb, RET_DK), F32)
    q_t = jnp.concatenate([q * gamma, pad], axis=0).T
    k_t = jnp.concatenate([k, pad], axis=0).T
    for s in range(sb):
        state = si_ref[0, s, 0]
        vs = v[s:s + 1, :]
        o_state = jnp.sum(q_t[:, s:s + 1] * state, axis=0, keepdims=True)
        so_ref[0, s, 0] = gamma * state + k_t[:, s:s + 1] * vs
        obuf[s:s + 1, :] = qk[s:s + 1, :] * vs + o_state
    u_ref[...] = _group_norm_gate(obuf[...], g_ref[...])


def _ret_sample(proj, cos_s, sin_s, state_ret, j, prev_out):
    sb = 8
    r0 = TP // sb
    aliased = prev_out is not None
    in_specs = [pl.BlockSpec((sb, RET_DK), lambda i, h: (r0 + i, h)),
                pl.BlockSpec((sb, RET_DK), lambda i, h: (r0 + i, RET_HEADS + h)),
                pl.BlockSpec((sb, RET_DV), lambda i, h: (r0 + i, RET_HEADS + h)),
                pl.BlockSpec((sb, RET_DV), lambda i, h: (r0 + i, 2 * RET_HEADS + h)),
                pl.BlockSpec((1, RET_DK // 2), lambda i, h: (0, 0)),
                pl.BlockSpec((1, RET_DK // 2), lambda i, h: (0, 0)),
                pl.BlockSpec((1, sb, 1, RET_DK, RET_DV), lambda i, h: (j, i, h, 0, 0))]
    args = [proj, proj, proj, proj, cos_s, sin_s, state_ret]
    aliases = {}
    if aliased:
        in_specs.append(pl.BlockSpec(memory_space=pl.ANY))
        args.append(prev_out)
        aliases = {len(args) - 1: 1}
    return pl.pallas_call(
        functools.partial(_ret_sample_body, sb=sb, aliased=aliased),
        out_shape=(jax.ShapeDtypeStruct((DEC_BATCH, RET_HV), F32),
                   jax.ShapeDtypeStruct(state_ret.shape, F32)),
        grid=(DEC_BATCH // sb, RET_HEADS),
        in_specs=in_specs,
        out_specs=(pl.BlockSpec((sb, RET_DV), lambda i, h: (i, h)),
                   pl.BlockSpec((1, sb, 1, RET_DK, RET_DV), lambda i, h: (j, i, h, 0, 0))),
        scratch_shapes=[pltpu.VMEM((sb, RET_DV), F32)],
        input_output_aliases=aliases,
        compiler_params=_cparams(("arbitrary", "arbitrary")),
        name="ret_sample",
    )(*args)


def _out_proj_body(u_ref, w_ref, x_ref, gt_ref, o_ref, *, per_row, tm):
    y = _bdot(u_ref[...].astype(BF16), w_ref[0].astype(BF16))
    if per_row:
        gate = gt_ref[0]
    else:
        gate = gt_ref[0, pl.ds(pl.program_id(0) * tm // SEQ, 1), :]
    o_ref[...] = x_ref[...] + gate * y


def _out_proj(u, w_out, j, x_all, row_tile0, mod, layer, per_row):
    m = u.shape[0]
    tm = min(m, 1024)
    tn = 512
    gate_blk = 2 * (D_MODEL // tn)
    mrows = mod.shape[1]
    return pl.pallas_call(
        functools.partial(_out_proj_body, per_row=per_row, tm=tm),
        out_shape=jax.ShapeDtypeStruct((m, D_MODEL), F32),
        grid=(m // tm, D_MODEL // tn),
        in_specs=[pl.BlockSpec((tm, RET_HV), lambda i, n: (i, 0)),
                  pl.BlockSpec((1, RET_HV, tn), lambda i, n: (j, 0, n)),
                  pl.BlockSpec((tm, tn), lambda i, n: (row_tile0 + i, n)),
                  pl.BlockSpec((1, mrows, tn), lambda i, n: (layer, 0, gate_blk + n))],
        out_specs=pl.BlockSpec((tm, tn), lambda i, n: (i, n)),
        compiler_params=_cparams(("arbitrary", "arbitrary")),
        name="ret_out_proj",
    )(u, w_out, x_all, mod)


def _ffn_pre_body(xp_ref, xs_ref, shp, scp, shs, scs, g_ref, wr_ref, br_ref,
                  hw_ref, idx_ref, gate_ref, rank_ref, cnt_ref, carry):
    i = pl.program_id(0)
    tm = TOK_TILE

    @pl.when(i == 0)
    def _():
        carry[...] = jnp.zeros_like(carry)

    x = jnp.where(i == N_PROMPT_TILES, xs_ref[...], xp_ref[...])
    h = _norm_mod(x, g_ref[...], _pick_mod(i, shp, shs), _pick_mod(i, scp, scs))

    bits = lax.bitcast_convert_type(h.astype(BF16).astype(F32), U32)
    half = D_MODEL // 2
    words = (bits[:, :half] >> 16) | (bits[:, half:] & jnp.uint32(0xFFFF0000))
    for a in range(PACK_ROWS):
        hw_ref[pl.ds(a, tm, stride=PACK_ROWS), :] = words[:, a * LANES:(a + 1) * LANES]

    lane = lax.broadcasted_iota(I32, (tm, LANES), 1)
    lane_f = lane.astype(F32)
    logits = jnp.dot(h, wr_ref[0], preferred_element_type=F32, precision=lax.Precision.HIGHEST) + br_ref[0]
    logits = jnp.where(lane < N_EXPERTS, logits, -jnp.inf)

    hots, vals = [], []
    idx_out = jnp.zeros((tm, LANES), F32)
    for kk in range(TOP_K):
        m = jnp.max(logits, axis=1, keepdims=True)
        pick = jnp.min(jnp.where(logits == m, lane_f, float(LANES)), axis=1, keepdims=True)
        hot = lane_f == pick
        logits = jnp.where(hot, -jnp.inf, logits)
        hots.append(hot)
        vals.append(m)
        idx_out = jnp.where(lane == kk, pick, idx_out)
    exps = [jnp.exp(v - vals[0]) for v in vals]
    den = exps[0] + exps[1] + exps[2] + exps[3]
    gate_out = jnp.zeros((tm, LANES), F32)
    for kk in range(TOP_K):
        gate_out = jnp.where(lane == kk, exps[kk] / den, gate_out)

    cnt = jnp.zeros((tm, LANES), F32)
    for hot in hots:
        cnt = cnt + hot.astype(F32)
    r_i = lax.broadcasted_iota(I32, (tm, tm), 0)
    c_i = lax.broadcasted_iota(I32, (tm, tm), 1)
    lower = jnp.where(c_i < r_i, 1.0, 0.0).astype(BF16)
    base = _bdot(lower, cnt.astype(BF16)) + carry[0:1, :]
    rank_out = jnp.zeros((tm, LANES), F32)
    for kk in range(TOP_K):
        rk = jnp.sum(jnp.where(hots[kk], base, 0.0), axis=1, keepdims=True)
        rank_out = jnp.where(lane == kk, rk, rank_out)
    total = carry[0:1, :] + jnp.sum(cnt, axis=0, keepdims=True)
    carry[0:1, :] = total

    idx_ref[...] = idx_out.astype(I32)
    gate_ref[...] = gate_out
    rank_ref[...] = rank_out.astype(I32)

    @pl.when(i == pl.num_programs(0) - 1)
    def _():
        cnt_ref[...] = jnp.broadcast_to(total, cnt_ref.shape).astype(I32)


def _ffn_pre(x_p, x_s, mod_p, mod_s, layer, g, w_router_pad, b_router_pad):
    tm = TOK_TILE
    tok = lambda i: (i, 0)
    return pl.pallas_call(
        _ffn_pre_body,
        out_shape=(jax.ShapeDtypeStruct((T_ALL * PACK_ROWS, LANES), U32),
                   jax.ShapeDtypeStruct((T_ALL, LANES), I32),
                   jax.ShapeDtypeStruct((T_ALL, LANES), F32),
                   jax.ShapeDtypeStruct((T_ALL, LANES), I32),
                   jax.ShapeDtypeStruct((8, LANES), I32)),
        grid=(N_TOK_TILES,),
        in_specs=[pl.BlockSpec((tm, D_MODEL), lambda i: (jnp.minimum(i, N_PROMPT_TILES - 1), 0)),
                  pl.BlockSpec((tm, D_MODEL), lambda i: (0, 0)),
                  pl.BlockSpec((1, 8, D_MODEL), lambda i: (layer, 0, 3)),
                  pl.BlockSpec((1, 8, D_MODEL), lambda i: (layer, 0, 4)),
                  pl.BlockSpec((1, DEC_BATCH, D_MODEL), lambda i: (layer, 0, 3)),
                  pl.BlockSpec((1, DEC_BATCH, D_MODEL), lambda i: (layer, 0, 4)),
                  pl.BlockSpec((1, D_MODEL), lambda i: (0, 0)),
                  pl.BlockSpec((1, D_MODEL, LANES), lambda i: (layer, 0, 0)),
                  pl.BlockSpec((1, 1, LANES), lambda i: (layer, 0, 0))],
        out_specs=(pl.BlockSpec((tm * PACK_ROWS, LANES), tok),
                   pl.BlockSpec((tm, LANES), tok),
                   pl.BlockSpec((tm, LANES), tok),
                   pl.BlockSpec((tm, LANES), tok),
                   pl.BlockSpec((8, LANES), lambda i: (0, 0))),
        scratch_shapes=[pltpu.VMEM((8, LANES), F32)],
        compiler_params=_cparams(("arbitrary",)),
        name="ffn_pre",
    )(x_p, x_s, mod_p, mod_p, mod_s, mod_s, g, w_router_pad, b_router_pad)


def _route(idx, rank, counts):
    e = idx[:, :TOP_K]
    r = rank[:, :TOP_K]
    cnt = counts[0, :N_EXPERTS]
    ntile = (cnt + MOE_SUB - 1) // MOE_SUB
    tile_end = jnp.cumsum(ntile)
    tile_start = tile_end - ntile
    pos = (tile_start[e] * MOE_SUB + r).astype(I32)
    tok = jnp.broadcast_to(jnp.arange(T_ALL, dtype=I32)[:, None], pos.shape)
    row_tok = jnp.zeros((MOE_ROWS,), I32).at[pos.reshape(-1)].set(tok.reshape(-1))
    npass = (ntile + MOE_NSUB - 1) // MOE_NSUB
    item_end = jnp.cumsum(npass)
    item_start = item_end - npass
    n_items = item_end[-1]
    it = jnp.arange(MOE_ITEMS + 1, dtype=I32)
    it_c = jnp.minimum(it, n_items - 1)
    item_e = jnp.minimum(jnp.searchsorted(item_end, it_c, side="right"), N_EXPERTS - 1).astype(I32)
    p = it_c - item_start[item_e]
    item_row0 = ((tile_start[item_e] + p * MOE_NSUB) * MOE_SUB).astype(I32)
    item_nsub = jnp.where(it < n_items, jnp.minimum(ntile[item_e] - p * MOE_NSUB, MOE_NSUB), 0).astype(I32)
    return pos.reshape(-1), row_tok, item_e, item_row0, item_nsub, n_items.reshape(1).astype(I32)


def _prefetch_cover(n_cur, n_nxt):
    n_both = jnp.minimum(n_cur, n_nxt)
    npair = jnp.maximum(n_cur - 1, 0) // 2
    pairs_cov = jnp.clip((n_both - 1) // 2, 0, npair)
    has_single = jnp.logical_and(n_cur >= 2, (n_cur - 1) % 2 == 1)
    single_cov = jnp.where(jnp.logical_and(has_single, n_both == n_cur), 1, 0)
    total = jnp.where(n_both >= 1, 1 + 2 * pairs_cov + single_cov, 0)
    return pairs_cov, single_cov, total


def _experts_body(item_e, item_row0, item_nsub, n_items, row_tok, hw_hbm, *refs):
    wg_refs, refs = refs[:MOE_WSPLIT], refs[MOE_WSPLIT:]
    wu_refs, refs = refs[:MOE_WSPLIT], refs[MOE_WSPLIT:]
    bg_ref, bu_ref, refs = refs[0], refs[1], refs[2:]
    wd_refs, refs = refs[:MOE_WSPLIT], refs[MOE_WSPLIT:]
    bd_ref, ys_hbm, xraw, hbuf, wgb, wub, wdb, ybuf, gsem, ysem, pend, pend_row, pend_col = refs

    def w_chunk(w_refs, c):
        per = D_MODEL // MOE_WSPLIT
        r = c * MOE_TN
        return w_refs[r // per][0, 0, r % per:r % per + MOE_TN, :]

    it = pl.program_id(0)
    j = pl.program_id(1)
    nsub = item_nsub[it]
    row0 = item_row0[it]
    nsub_next = item_nsub[it + 1]
    row0_next = item_row0[it + 1]
    n_pref = jnp.where(it > 0, _prefetch_cover(item_nsub[jnp.maximum(it - 1, 0)], nsub)[2], 0)
    npair = jnp.maximum(nsub - 1, 0) // 2
    has_single = jnp.logical_and(nsub >= 2, (nsub - 1) % 2 == 1)
    active = it < n_items[0]
    ts = MOE_SUB
    tn = MOE_TN
    pf_rows = ts // MOE_JD

    def chunk(c):
        return slice(c * tn, (c + 1) * tn)

    def row_copy(base, r, s):
        src = hw_hbm.at[pl.ds(pl.multiple_of(row_tok[base + r] * PACK_ROWS, PACK_ROWS), PACK_ROWS)]
        dst = xraw.at[pl.ds(pl.multiple_of(r * PACK_ROWS, PACK_ROWS), PACK_ROWS)]
        return pltpu.make_async_copy(src, dst, gsem.at[s])

    def out_copy(row, slot, col, m):
        dst = ys_hbm.at[pl.ds(pl.multiple_of(row, ts), m * ts), pl.ds(pl.multiple_of(col * tn, tn), tn)]
        return pltpu.make_async_copy(ybuf.at[slot, pl.ds(0, m * ts)], dst, ysem.at[slot])

    def out_wait(slot):
        for m in (1, 2):
            @pl.when(pend[slot] == m)
            def _():
                out_copy(pend_row[slot], slot, pend_col[slot], m).wait()
                pend[slot] = 0

    def out_start(row, slot, col, m):
        out_copy(row, slot, col, m).start()
        pend[slot] = m
        pend_row[slot] = row
        pend_col[slot] = col

    @pl.when(jnp.logical_and(it == 0, j == 0))
    def _():
        pend[0] = 0
        pend[1] = 0

    def swiglu(gt, up):
        gt = jnp.minimum(gt, SWIGLU_LIMIT)
        up = jnp.clip(up, -SWIGLU_LIMIT, SWIGLU_LIMIT)
        return ((up + 1.0) * gt * jax.nn.sigmoid(SWIGLU_ALPHA * gt)).astype(BF16)

    @pl.when(jnp.logical_and(active, j == 0))
    def _gather():
        def issue(r, c):
            row_copy(row0, r, lax.shift_right_logical(r, int(np.log2(ts)))).start()
            return c
        lax.fori_loop(n_pref * ts, nsub * ts, issue, 0)

        def land(s, c):
            rows = pl.ds(pl.multiple_of(s * ts * PACK_ROWS, ts * PACK_ROWS), ts * PACK_ROWS)
            pltpu.make_async_copy(hw_hbm.at[pl.ds(0, ts * PACK_ROWS)], xraw.at[rows], gsem.at[s]).wait()
            return c
        lax.fori_loop(0, nsub, land, 0)

    def unpack(w, high):
        bits = (w & jnp.uint32(0xFFFF0000)) if high else (w << 16)
        return lax.bitcast_convert_type(bits, F32).astype(BF16)

    def x_cols(r0, lane_blocks, high):
        parts = [unpack(xraw[pl.ds(r0 * PACK_ROWS + a, ts, stride=PACK_ROWS), :], high) for a in lane_blocks]
        return jnp.concatenate(parts, axis=1)

    @pl.when(jnp.logical_and(active, j < MOE_JGU))
    def _gate_up():
        per_half = (D_MODEL // 2) // tn
        gt = jnp.broadcast_to(bg_ref[0, 0], (ts, tn))
        up = jnp.broadcast_to(bu_ref[0, 0], (ts, tn))
        for c in range(D_MODEL // tn):
            wg_c = w_chunk(wg_refs, c).astype(BF16)
            wu_c = w_chunk(wu_refs, c).astype(BF16)
            wgb[chunk(c), :] = wg_c
            wub[chunk(c), :] = wu_c
            blocks = range((c % per_half) * (tn // LANES), (c % per_half + 1) * (tn // LANES))
            x_c = x_cols(0, blocks, c >= per_half)
            gt = gt + _bdot(x_c, wg_c)
            up = up + _bdot(x_c, wu_c)
        hbuf[j, 0:ts, :] = swiglu(gt, up)

        def gu_tile(s, m):
            r0 = pl.multiple_of(s * ts, ts)
            ws = [xraw[pl.ds(r0 * PACK_ROWS + a, m * ts, stride=PACK_ROWS), :] for a in range(PACK_ROWS)]
            x = jnp.concatenate([unpack(w, False) for w in ws] + [unpack(w, True) for w in ws], axis=1)
            hbuf[j, pl.ds(r0, m * ts), :] = swiglu(_bdot(x, wgb[...]) + bg_ref[0, 0],
                                                   _bdot(x, wub[...]) + bu_ref[0, 0])

        def pair(p, c):
            gu_tile(1 + 2 * p, 2)
            return c
        lax.fori_loop(0, npair, pair, 0)

        @pl.when(has_single)
        def _():
            gu_tile(nsub - 1, 1)

    @pl.when(jnp.logical_and(active, j >= MOE_JGU))
    def _down():
        jd = j - MOE_JGU
        pairs_cov, single_cov, _ = _prefetch_cover(nsub, nsub_next)

        def down_tile(s, m, slot, first, prefetch):
            rows = slice(0, ts) if first else pl.ds(pl.multiple_of(s * ts, ts), m * ts)
            out_wait(slot)
            acc = jnp.broadcast_to(bd_ref[0, 0], (m * ts, tn))
            for c in range(MOE_JGU):
                if first:
                    wd_c = w_chunk(wd_refs, c).astype(BF16)
                    wdb[chunk(c), :] = wd_c
                else:
                    wd_c = wdb[chunk(c), :]
                acc = acc + _bdot(hbuf[c, rows, :], wd_c)
            ybuf[slot, 0:m * ts, :] = acc
            out_start(row0 + s * ts, slot, jd, m)
            if prefetch:
                for mm in range(m):
                    for q in range(pf_rows):
                        row_copy(row0_next, (s + mm) * ts + jd * pf_rows + q, s + mm).start()

        @pl.when(nsub_next >= 1)
        def _():
            down_tile(0, 1, 0, True, True)

        @pl.when(nsub_next < 1)
        def _():
            down_tile(0, 1, 0, True, False)

        def pair_pf(p, c):
            down_tile(1 + 2 * p, 2, (1 + p) % 2, False, True)
            return c
        lax.fori_loop(0, pairs_cov, pair_pf, 0)

        def pair(p, c):
            down_tile(1 + 2 * p, 2, (1 + p) % 2, False, False)
            return c
        lax.fori_loop(pairs_cov, npair, pair, 0)

        @pl.when(jnp.logical_and(has_single, single_cov == 1))
        def _():
            down_tile(nsub - 1, 1, (1 + npair) % 2, False, True)

        @pl.when(jnp.logical_and(has_single, single_cov == 0))
        def _():
            down_tile(nsub - 1, 1, (1 + npair) % 2, False, False)

    @pl.when(jnp.logical_and(it == pl.num_programs(0) - 1, j == pl.num_programs(1) - 1))
    def _fill_tail():
        out_wait(0)
        out_wait(1)
        ybuf[0, 0:ts, :] = jnp.zeros((ts, tn), F32)
        last = n_items[0] - 1
        used = item_row0[last] // ts + item_nsub[last]

        def tail_copy(t, c):
            dst = ys_hbm.at[pl.ds(pl.multiple_of(t * ts, ts), ts), pl.ds(c * tn, tn)]
            return pltpu.make_async_copy(ybuf.at[0, pl.ds(0, ts)], dst, ysem.at[0])

        def fill(t, carry):
            for c in range(MOE_JD):
                tail_copy(t, c).start()
            for c in range(MOE_JD):
                tail_copy(t, c).wait()
            return carry
        lax.fori_loop(used, MOE_ROW_TILES, fill, 0)


def _experts(route, hw, w_gate_up, b_gate_up, w_down, b_down, layer):
    _, row_tok, item_e, item_row0, item_nsub, n_items = route
    tn = MOE_TN
    jgu = MOE_JGU
    gu_c = lambda j: jnp.minimum(j, jgu - 1)
    dn_c = lambda j: jnp.maximum(j - jgu, 0)
    wrows = D_MODEL // MOE_WSPLIT
    grid_spec = pltpu.PrefetchScalarGridSpec(
        num_scalar_prefetch=5,
        grid=(MOE_ITEMS, MOE_JGU + MOE_JD),
        in_specs=(
            [pl.BlockSpec(memory_space=pl.ANY)]
            + [pl.BlockSpec((1, 1, wrows, tn), lambda it, j, ie, *_, q=q: (layer, ie[it], q, gu_c(j)))
               for q in range(MOE_WSPLIT)]
            + [pl.BlockSpec((1, 1, wrows, tn), lambda it, j, ie, *_, q=q: (layer, ie[it], q, jgu + gu_c(j)))
               for q in range(MOE_WSPLIT)]
            + [pl.BlockSpec((1, 1, 1, tn), lambda it, j, ie, *_: (layer, ie[it], 0, gu_c(j))),
               pl.BlockSpec((1, 1, 1, tn), lambda it, j, ie, *_: (layer, ie[it], 0, jgu + gu_c(j)))]
            + [pl.BlockSpec((1, 1, wrows, tn), lambda it, j, ie, *_, q=q: (layer, ie[it], q, dn_c(j)))
               for q in range(MOE_WSPLIT)]
            + [pl.BlockSpec((1, 1, 1, tn), lambda it, j, ie, *_: (layer, ie[it], 0, dn_c(j)))]
        ),
        out_specs=pl.BlockSpec(memory_space=pl.ANY),
        scratch_shapes=[
            pltpu.VMEM((MOE_NSUB * MOE_SUB * PACK_ROWS, LANES), U32),
            pltpu.VMEM((MOE_JGU, MOE_NSUB * MOE_SUB, tn), BF16),
            pltpu.VMEM((D_MODEL, tn), BF16),
            pltpu.VMEM((D_MODEL, tn), BF16),
            pltpu.VMEM((D_FF, tn), BF16),
            pltpu.VMEM((2, 2 * MOE_SUB, tn), F32),
            pltpu.SemaphoreType.DMA((MOE_NSUB,)),
            pltpu.SemaphoreType.DMA((2,)),
            pltpu.SMEM((2,), I32),
            pltpu.SMEM((2,), I32),
            pltpu.SMEM((2,), I32),
        ],
    )
    ne = w_gate_up.shape[1]
    return pl.pallas_call(
        _experts_body,
        out_shape=jax.ShapeDtypeStruct((MOE_ROWS, D_MODEL), F32),
        grid_spec=grid_spec,
        compiler_params=_cparams(("arbitrary", "arbitrary")),
        name="moe_experts",
    )(item_e, item_row0, item_nsub, n_items, row_tok,
      hw, *([w_gate_up] * (2 * MOE_WSPLIT)),
      b_gate_up.reshape(DEPTH, ne, 1, 2 * D_FF), b_gate_up.reshape(DEPTH, ne, 1, 2 * D_FF),
      *([w_down] * MOE_WSPLIT), b_down.reshape(DEPTH, ne, 1, D_MODEL))


def _combine_body(pos, ys_hbm, xp_ref, xs_ref, gate_ref, gtp, gts, o_ref, ybuf, sem):
    i = pl.program_id(0)
    tm = TOK_TILE

    def issue_tile(tile, slot):
        base = tile * (tm * TOP_K)

        def issue(t, c):
            for kk in range(TOP_K):
                pltpu.make_async_copy(ys_hbm.at[pos[base + t * TOP_K + kk]], ybuf.at[slot, kk, t],
                                      sem.at[slot]).start()
            return c
        lax.fori_loop(0, tm, issue, 0)

    @pl.when(i == 0)
    def _():
        issue_tile(0, 0)

    @pl.when(i + 1 < pl.num_programs(0))
    def _():
        issue_tile(i + 1, (i + 1) % 2)

    slot = i % 2
    for kk in range(TOP_K):
        pltpu.make_async_copy(ys_hbm.at[pl.ds(0, tm)], ybuf.at[slot, kk], sem.at[slot]).wait()

    gates = gate_ref[...]
    f = gates[:, 0:1] * ybuf[slot, 0]
    for kk in range(1, TOP_K):
        f = f + gates[:, kk:kk + 1] * ybuf[slot, kk]
    x = jnp.where(i == N_PROMPT_TILES, xs_ref[...], xp_ref[...])
    o_ref[...] = x + _pick_mod(i, gtp, gts) * f


def _combine(pos, ys, x_p, x_s, gates, mod_p, mod_s, layer):
    tm = TOK_TILE
    grid_spec = pltpu.PrefetchScalarGridSpec(
        num_scalar_prefetch=1,
        grid=(N_TOK_TILES,),
        in_specs=[pl.BlockSpec(memory_space=pl.ANY),
                  pl.BlockSpec((tm, D_MODEL), lambda i, *_: (jnp.minimum(i, N_PROMPT_TILES - 1), 0)),
                  pl.BlockSpec((tm, D_MODEL), lambda i, *_: (0, 0)),
                  pl.BlockSpec((tm, LANES), lambda i, *_: (i, 0)),
                  pl.BlockSpec((1, 8, D_MODEL), lambda i, *_: (layer, 0, 5)),
                  pl.BlockSpec((1, DEC_BATCH, D_MODEL), lambda i, *_: (layer, 0, 5))],
        out_specs=pl.BlockSpec((tm, D_MODEL), lambda i, *_: (i, 0)),
        scratch_shapes=[pltpu.VMEM((2, TOP_K, tm, D_MODEL), F32), pltpu.SemaphoreType.DMA((2,))],
    )
    return pl.pallas_call(
        _combine_body,
        out_shape=jax.ShapeDtypeStruct((T_ALL, D_MODEL), F32),
        grid_spec=grid_spec,
        compiler_params=_cparams(("arbitrary",)),
        name="moe_combine",
    )(pos, ys, x_p, x_s, gates, mod_p, mod_s)


def _final_norm_body(x_ref, g_ref, op_ref, os_ref):
    i = pl.program_id(0)
    x = x_ref[...]
    y = x * lax.rsqrt(jnp.mean(x * x, axis=-1, keepdims=True) + NORM_EPS) * g_ref[...]

    @pl.when(i < N_PROMPT_TILES)
    def _():
        op_ref[...] = y

    @pl.when(i == N_PROMPT_TILES)
    def _():
        os_ref[...] = y


def _final_norm(x_all, g):
    tm = TOK_TILE
    return pl.pallas_call(
        _final_norm_body,
        out_shape=(jax.ShapeDtypeStruct((TP, D_MODEL), F32),
                   jax.ShapeDtypeStruct((DEC_BATCH, D_MODEL), F32)),
        grid=(N_TOK_TILES,),
        in_specs=[pl.BlockSpec((tm, D_MODEL), lambda i: (i, 0)),
                  pl.BlockSpec((1, D_MODEL), lambda i: (0, 0))],
        out_specs=(pl.BlockSpec((tm, D_MODEL), lambda i: (jnp.minimum(i, N_PROMPT_TILES - 1), 0)),
                   pl.BlockSpec((tm, D_MODEL), lambda i: (0, 0))),
        compiler_params=_cparams(("arbitrary",)),
        name="final_norm",
    )(x_all, g)


def _rope_tables(pos):
    inv = ROPE_BASE ** (-jnp.linspace(0.0, 1.0, RET_DK // 2, dtype=F32))
    ang = pos.astype(F32)[:, None] * inv[None, :]
    return jnp.cos(ang), jnp.sin(ang)


def kernel(x_prompt, x_sample, state_pool, state_ret, c_prompt, c_sample, w_mod, b_mod, g_norm_mix, g_norm_ffn,
           pool_w, pool_scale, ret_w_in, ret_w_out, moe_w_router, moe_b_router, moe_w_gate_up, moe_b_gate_up,
           moe_w_down, moe_b_down, g_final):
    assert x_prompt.shape == (BATCH, SEQ, D_MODEL) and x_sample.shape == (DEC_BATCH, 1, D_MODEL)
    c_prompt8 = jnp.concatenate([c_prompt, jnp.zeros((8 - BATCH, D_MODEL), F32)], axis=0)
    mod_s, mod_p = _adaln(c_sample, c_prompt8, w_mod, b_mod)
    cos_p, sin_p = _rope_tables(jnp.arange(SEQ))
    cos_s, sin_s = _rope_tables(PAST_LEN + jnp.arange(1))
    w_router_pad = jnp.pad(moe_w_router, ((0, 0), (0, 0), (0, LANES - N_EXPERTS)))
    b_router_pad = jnp.pad(moe_b_router, ((0, 0), (0, LANES - N_EXPERTS))).reshape(DEPTH, 1, LANES)

    x_p_src, x_p_tile0 = x_prompt.reshape(TP, D_MODEL), 0
    x_s_src, x_s_rows0 = x_sample.reshape(DEC_BATCH, D_MODEL), 0
    x_all = None
    new_pool_p, new_pool_s, new_ret_p = [], [], []
    ret_s_out = None
    for layer in range(DEPTH):
        j = layer // N_MIXERS
        g_mix = g_norm_mix[layer].reshape(1, D_MODEL)
        g_ffn = g_norm_ffn[layer].reshape(1, D_MODEL)
        if layer % N_MIXERS == 0:
            scale = pool_scale[j].reshape(1, D_MODEL)
            x_p, pool_p = _pool_prompt(x_p_src, x_p_tile0, mod_p, layer, g_mix, pool_w, scale, j)
            x_s, pool_s = _pool_sample(x_s_src, x_s_rows0 // 16, state_pool, mod_s, layer, g_mix, pool_w, scale, j)
            new_pool_p.append(pool_p)
            new_pool_s.append(pool_s)
        else:
            h_all = _norm_mod_all(x_all, mod_p, mod_s, layer, g_mix)
            proj = _in_proj(h_all, ret_w_in, j)
            u_p, ret_p = _ret_prompt(proj, cos_p, sin_p)
            u_s, ret_s_out = _ret_sample(proj, cos_s, sin_s, state_ret, j, ret_s_out)
            new_ret_p.append(ret_p)
            x_p = _out_proj(u_p, ret_w_out, j, x_all, 0, mod_p, layer, per_row=False)
            x_s = _out_proj(u_s, ret_w_out, j, x_all, TP // DEC_BATCH, mod_s, layer, per_row=True)
        hw, idx, gates, rank, counts = _ffn_pre(x_p, x_s, mod_p, mod_s, layer, g_ffn, w_router_pad, b_router_pad)
        route = _route(idx, rank, counts)
        ys = _experts(route, hw, moe_w_gate_up, moe_b_gate_up, moe_w_down, moe_b_down, layer)
        x_all = _combine(route[0], ys, x_p, x_s, gates, mod_p, mod_s, layer)
        x_p_src, x_p_tile0 = x_all, 0
        x_s_src, x_s_rows0 = x_all, TP
    y_p, y_s = _final_norm(x_all, g_final.reshape(1, D_MODEL))
    return (y_p.reshape(BATCH, SEQ, D_MODEL), y_s.reshape(DEC_BATCH, 1, D_MODEL),
            jnp.stack(new_pool_p), jnp.stack(new_pool_s), jnp.stack(new_ret_p), ret_s_out)
```
